```python
import math
import jax, jax.numpy as jnp
from jax import lax
import numpy as np

D_MODEL = 4096
BATCH = 4
SEQ = 2048
DEPTH = 1

HEAD_DIM = 128
N_HEADS_TOTAL = D_MODEL // HEAD_DIM
H_MOBA = N_HEADS_TOTAL // 2
H_DSA = N_HEADS_TOTAL - H_MOBA
W_MOBA = H_MOBA * HEAD_DIM
W_DSA = H_DSA * HEAD_DIM
MOBA_BLOCK = 256
MOBA_TOPK = 3
MOBA_Q_CHUNK = 16
IDX_HEADS = 16
IDX_DIM = 64
DSA_TOPK = 256
DSA_Q_CHUNK = 32
NUM_BUCKETS = 32
MAX_DISTANCE = 128
D_FF = 256 * ((8 * D_MODEL // 3 + 255) // 256)
CONV_WIDTH = 3
ALPHA = (2.0 * DEPTH) ** 0.25
BETA = (8.0 * DEPTH) ** -0.25
NEG = -1e30
LN_EPS = 1e-5

_SIZES = (W_MOBA, W_MOBA, W_MOBA, W_DSA, W_DSA, W_DSA,
          IDX_HEADS * IDX_DIM, IDX_DIM, IDX_HEADS)
IN_COLS = sum(_SIZES)
SPLIT_POINTS = tuple(sum(_SIZES[:i + 1]) for i in range(len(_SIZES) - 1))

kernel_name = "hymba_moba_dsa_deepnorm_layer"


def layer_norm(x, g, b):
    xf = x.astype(jnp.float32)
    mu = jnp.mean(xf, axis=-1, keepdims=True)
    var = jnp.mean(jnp.square(xf - mu), axis=-1, keepdims=True)
    y = (xf - mu) * lax.rsqrt(var + LN_EPS) * g.astype(jnp.float32) + b.astype(jnp.float32)
    return y.astype(x.dtype)


def rms_norm(x, g):
    xf = x.astype(jnp.float32)
    y = xf * lax.rsqrt(jnp.mean(jnp.square(xf), axis=-1, keepdims=True) + LN_EPS)
    return (y * g.astype(jnp.float32)).astype(x.dtype)


def t5_bucket(dist):
    d = jnp.maximum(dist, 0)
    max_exact = NUM_BUCKETS // 2
    df = jnp.maximum(d, max_exact).astype(jnp.float32)
    large = max_exact + (jnp.log(df / max_exact) / math.log(MAX_DISTANCE / max_exact)
                         * (NUM_BUCKETS - max_exact)).astype(jnp.int32)
    large = jnp.minimum(large, NUM_BUCKETS - 1)
    return jnp.where(d < max_exact, d, large)


def rel_bias(dist, tab):
    bucket = t5_bucket(dist)
    h = jnp.arange(tab.shape[0]).reshape((1, -1) + (1,) * (bucket.ndim - 2))
    return tab[h, bucket].astype(jnp.float32)


def chunk_axis(a, axis, size):
    shape = a.shape
    n = shape[axis] // size
    a = a.reshape(shape[:axis] + (n, size) + shape[axis + 1:])
    return jnp.moveaxis(a, axis, 0)


def moba_attention(q, k, v, tab):
    B, H, S, Dh = q.shape
    nb = -(-S // MOBA_BLOCK)
    pad = nb * MOBA_BLOCK - S
    kb = jnp.pad(k, ((0, 0), (0, 0), (0, pad), (0, 0))).reshape(B, H, nb, MOBA_BLOCK, Dh)
    vb = jnp.pad(v, ((0, 0), (0, 0), (0, pad), (0, 0))).reshape(B, H, nb, MOBA_BLOCK, Dh)
    kbar = jnp.mean(kb.astype(jnp.float32), axis=3)
    pos = jnp.arange(S, dtype=jnp.int32)
    qblk = pos // MOBA_BLOCK
    gate = jnp.einsum('bhsd,bhnd->bhsn', q.astype(jnp.float32), kbar)
    past = jnp.arange(nb)[None, :] < qblk[:, None]
    gate = jnp.where(past[None, None], gate, NEG)
    n_sel = min(MOBA_TOPK, nb)
    _, top = lax.top_k(gate, n_sel)
    own = jnp.broadcast_to(qblk, (B, H, S))[..., None]
    sel = jnp.concatenate([top.astype(jnp.int32), own], axis=-1)
    slot_ok = jnp.concatenate([jnp.arange(n_sel)[None, :] < qblk[:, None],
                               jnp.ones((S, 1), dtype=bool)], axis=-1)
    scale = HEAD_DIM ** -0.5
    bi = jnp.arange(B)[:, None, None, None]
    hi = jnp.arange(H)[None, :, None, None]
    offs = jnp.arange(MOBA_BLOCK, dtype=jnp.int32)

    def step(args):
        q_c, sel_c, t_c, ok_c = args
        kg = kb[bi, hi, sel_c]
        vg = vb[bi, hi, sel_c]
        logits = jnp.einsum('bhqd,bhqjkd->bhqjk', q_c, kg).astype(jnp.float32) * scale
        kpos = sel_c[..., None] * MOBA_BLOCK + offs
        dist = t_c[None, None, :, None, None] - kpos
        logits = logits + rel_bias(dist, tab)
        ok = ok_c[None, None, :, :, None] & (dist >= 0)
        logits = jnp.where(ok, logits, NEG)
        shp = logits.shape
        p = jax.nn.softmax(logits.reshape(shp[:3] + (-1,)), axis=-1).reshape(shp)
        return jnp.einsum('bhqjk,bhqjkd->bhqd', p.astype(v.dtype), vg)

    out = lax.map(step, (chunk_axis(q, 2, MOBA_Q_CHUNK), chunk_axis(sel, 2, MOBA_Q_CHUNK),
                         chunk_axis(pos, 0, MOBA_Q_CHUNK), chunk_axis(slot_ok, 0, MOBA_Q_CHUNK)))
    return jnp.moveaxis(out, 0, 2).reshape(B, H, S, Dh)


def dsa_attention(q, k, v, q_idx, k_idx, w_idx, tab):
    B, H, S, Dh = q.shape
    n_top = min(DSA_TOPK, S // 4)
    pos = jnp.arange(S, dtype=jnp.int32)
    scale = HEAD_DIM ** -0.5
    bi = jnp.arange(B)[:, None, None, None]
    hi = jnp.arange(H)[None, :, None, None]

    def step(args):
        q_c, qi_c, wi_c, t_c = args
        s = jnp.einsum('bqhd,bsd->bqsh', qi_c, k_idx).astype(jnp.float32) * IDX_DIM ** -0.5
        w = wi_c.astype(jnp.float32) * IDX_HEADS ** -0.5
        score = jnp.einsum('bqsh,bqh->bqs', jax.nn.relu(s), w)
        admissible = pos[None, :] <= t_c[:, None]
        score = jnp.where(admissible[None], score, NEG)
        _, sel = lax.top_k(score, n_top)
        sel = sel.astype(jnp.int32)
        kg = k[bi, hi, sel[:, None]]
        vg = v[bi, hi, sel[:, None]]
        logits = jnp.einsum('bhqd,bhqkd->bhqk', q_c, kg).astype(jnp.float32) * scale
        dist = (t_c[None, :, None] - sel)[:, None]
        logits = logits + rel_bias(dist, tab)
        logits = jnp.where(dist >= 0, logits, NEG)
        p = jax.nn.softmax(logits, axis=-1)
        return jnp.einsum('bhqk,bhqkd->bhqd', p.astype(v.dtype), vg)

    out = lax.map(step, (chunk_axis(q, 2, DSA_Q_CHUNK), chunk_axis(q_idx, 1, DSA_Q_CHUNK),
                         chunk_axis(w_idx, 1, DSA_Q_CHUNK), chunk_axis(pos, 0, DSA_Q_CHUNK)))
    return jnp.moveaxis(out, 0, 2).reshape(B, H, S, Dh)


def setup_inputs(seed: int = 0) -> dict:
    key = jax.random.key(seed)
    ks = jax.random.split(key, 16)
    f32 = jnp.float32
    x = jax.random.normal(ks[0], (BATCH, SEQ, D_MODEL), f32)
    col_scale = jnp.concatenate([
        jnp.ones((2 * W_MOBA,), f32), jnp.full((W_MOBA,), BETA, f32),
        jnp.ones((2 * W_DSA,), f32), jnp.full((W_DSA,), BETA, f32),
        jnp.ones((IDX_HEADS * IDX_DIM + IDX_DIM + IDX_HEADS,), f32)])
    w_in = jax.random.normal(ks[1], (D_MODEL, IN_COLS), f32) * D_MODEL ** -0.5 * col_scale
    rel_table = 0.5 * jax.random.normal(ks[2], (NUM_BUCKETS, N_HEADS_TOTAL), f32)
    g_moba = 1.0 + 0.01 * jax.random.normal(ks[3], (W_MOBA,), f32)
    g_dsa = 1.0 + 0.01 * jax.random.normal(ks[4], (W_DSA,), f32)
    w_out = jax.random.normal(ks[5], (D_MODEL, D_MODEL), f32) * D_MODEL ** -0.5 * BETA
    ln1_g = 1.0 + 0.01 * jax.random.normal(ks[6], (D_MODEL,), f32)
    ln1_b = 0.01 * jax.random.normal(ks[7], (D_MODEL,), f32)
    w_up = jax.random.normal(ks[8], (D_MODEL, 2 * D_FF), f32) * D_MODEL ** -0.5 * BETA
    conv_w = jax.random.normal(ks[9], (CONV_WIDTH, 2 * D_FF), f32) * CONV_WIDTH ** -0.5
    conv_b = 0.01 * jax.random.normal(ks[10], (2 * D_FF,), f32)
    w_down = jax.random.normal(ks[11], (D_FF, D_MODEL), f32) * D_FF ** -0.5 * BETA
    ln2_g = 1.0 + 0.01 * jax.random.normal(ks[12], (D_MODEL,), f32)
    ln2_b = 0.01 * jax.random.normal(ks[13], (D_MODEL,), f32)
    return {"x": x, "w_in": w_in, "rel_table": rel_table, "g_moba": g_moba,
            "g_dsa": g_dsa, "w_out": w_out, "ln1_g": ln1_g, "ln1_b": ln1_b,
            "w_up": w_up, "conv_w": conv_w, "conv_b": conv_b, "w_down": w_down,
            "ln2_g": ln2_g, "ln2_b": ln2_b}


def reference(x, w_in, rel_table, g_moba, g_dsa, w_out, ln1_g, ln1_b,
              w_up, conv_w, conv_b, w_down, ln2_g, ln2_b):
    B, S, _ = x.shape

    def heads(t, n):
        return t.reshape(B, S, n, HEAD_DIM).transpose(0, 2, 1, 3)

    def merge(t):
        return t.transpose(0, 2, 1, 3).reshape(B, S, -1)

    for _layer in range(DEPTH):
        proj = x @ w_in
        q_a, k_a, v_a, q_b, k_b, v_b, qi, ki, wi = jnp.split(proj, SPLIT_POINTS, axis=-1)
        y_a = moba_attention(heads(q_a, H_MOBA), heads(k_a, H_MOBA), heads(v_a, H_MOBA),
                             rel_table[:, :H_MOBA].T)
        y_b = dsa_attention(heads(q_b, H_DSA), heads(k_b, H_DSA), heads(v_b, H_DSA),
                            qi.reshape(B, S, IDX_HEADS, IDX_DIM), ki, wi,
                            rel_table[:, H_MOBA:].T)
        mixed = jnp.concatenate([rms_norm(merge(y_a), g_moba),
                                 rms_norm(merge(y_b), g_dsa)], axis=-1) @ w_out
        x = layer_norm(ALPHA * x + mixed, ln1_g, ln1_b)

        u = x @ w_up
        up = jnp.pad(u, ((0, 0), (CONV_WIDTH - 1, 0), (0, 0)))
        u = sum(conv_w[j] * up[:, j:j + S] for j in range(CONV_WIDTH)) + conv_b
        val, gate = jnp.split(u, 2, axis=-1)
        ffn = (val * jax.nn.silu(gate)) @ w_down
        x = layer_norm(ALPHA * x + ffn, ln2_g, ln2_b)
    return x
```

```python
import functools
import math

import jax
import jax.numpy as jnp
from jax import lax
from jax.experimental import pallas as pl
from jax.experimental.pallas import tpu as pltpu

F32 = jnp.float32
BF16 = jnp.bfloat16
I32 = jnp.int32

HEAD_DIM = 128
MOBA_BLOCK = 256
MOBA_TOPK = 3
IDX_HEADS = 16
IDX_DIM = 64
DSA_TOPK = 256
NUM_BUCKETS = 32
MAX_DISTANCE = 128
CONV_WIDTH = 3
DEPTH = 1
ALPHA = (2.0 * DEPTH) ** 0.25
NEG = -1e30
LN_EPS = 1e-5

TQ = MOBA_BLOCK
LANES = 128
VMEM_LIMIT = 56 * 1024 * 1024

NT_DIMS = (((1,), (1,)), ((), ()))
INT_MIN = -(2 ** 31)


def _params(n_axes, vmem=VMEM_LIMIT):
    return pltpu.CompilerParams(dimension_semantics=("arbitrary",) * n_axes,
                                vmem_limit_bytes=vmem)


def _pick(n, candidates):
    for c in candidates:
        if n % c == 0:
            return c
    raise ValueError(f"no tile for {n} among {candidates}")


def _proj_main_kernel(x_ref, w_ref, o_ref, wbf_ref):
    @pl.when(pl.program_id(1) == 0)
    def _():
        wbf_ref[...] = w_ref[...].astype(BF16)

    acc = jnp.dot(x_ref[...], wbf_ref[...], preferred_element_type=F32)
    for c in range(o_ref.shape[0]):
        o_ref[c] = acc[:, c * HEAD_DIM:(c + 1) * HEAD_DIM].astype(BF16)


def _proj_main(x_bf, w_in, n_cols):
    m, d = x_bf.shape
    tm = _pick(m, (1024, 512, 256))
    tn = _pick(n_cols, (512, 256, 128))
    hp = tn // HEAD_DIM
    return pl.pallas_call(
        _proj_main_kernel,
        grid=(n_cols // tn, m // tm),
        in_specs=[pl.BlockSpec((tm, d), lambda j, i: (i, 0)),
                  pl.BlockSpec((d, tn), lambda j, i: (0, j))],
        out_specs=pl.BlockSpec((hp, tm, HEAD_DIM), lambda j, i: (j, i, 0)),
        out_shape=jax.ShapeDtypeStruct((n_cols // HEAD_DIM, m, HEAD_DIM), BF16),
        scratch_shapes=[pltpu.VMEM((d, tn), BF16)],
        compiler_params=_params(2),
        name="proj_main",
    )(x_bf, w_in)


def _proj_idx_kernel(x_ref, w_ref, qi_ref, ki_ref, wi_ref):
    nq = IDX_HEADS * IDX_DIM
    acc = jnp.dot(x_ref[...], w_ref[...], preferred_element_type=F32)
    qi_ref[...] = (acc[:, :nq] * IDX_DIM ** -0.5).astype(BF16)
    ki_ref[...] = acc[:, nq:nq + IDX_DIM].astype(BF16)
    wi_ref[...] = acc[:, nq + IDX_DIM:nq + IDX_DIM + IDX_HEADS] * IDX_HEADS ** -0.5


def _proj_idx(x_bf, w_idx_bf):
    m, d = x_bf.shape
    ncols = w_idx_bf.shape[1]
    tm = _pick(m, (1024, 512, 256))
    nq = IDX_HEADS * IDX_DIM
    return pl.pallas_call(
        _proj_idx_kernel,
        grid=(m // tm,),
        in_specs=[pl.BlockSpec((tm, d), lambda i: (i, 0)),
                  pl.BlockSpec((d, ncols), lambda i: (0, 0))],
        out_specs=[pl.BlockSpec((tm, nq), lambda i: (i, 0)),
                   pl.BlockSpec((tm, IDX_DIM), lambda i: (i, 0)),
                   pl.BlockSpec((tm, IDX_HEADS), lambda i: (i, 0))],
        out_shape=[jax.ShapeDtypeStruct((m, nq), BF16),
                   jax.ShapeDtypeStruct((m, IDX_DIM), BF16),
                   jax.ShapeDtypeStruct((m, IDX_HEADS), F32)],
        compiler_params=_params(1),
        name="proj_idx",
    )(x_bf, w_idx_bf)


def _bias_kernel(tab_ref, o_ref):
    h = pl.program_id(0)
    r = lax.broadcasted_iota(I32, (TQ, TQ), 0)
    c = lax.broadcasted_iota(I32, (TQ, TQ), 1)
    max_exact = NUM_BUCKETS // 2
    for slot in range(3):
        dist = (2 - slot) * TQ + r - c
        d = jnp.maximum(dist, 0)
        df = jnp.maximum(d, max_exact).astype(F32)
        large = max_exact + (jnp.log(df / max_exact) / math.log(MAX_DISTANCE / max_exact)
                             * (NUM_BUCKETS - max_exact)).astype(I32)
        large = jnp.minimum(large, NUM_BUCKETS - 1)
        bucket = jnp.where(d < max_exact, d, large)
        out = jnp.zeros((TQ, TQ), F32)
        for k in range(NUM_BUCKETS):
            out = jnp.where(bucket == k, tab_ref[k, h], out)
        o_ref[0, slot] = jnp.where(dist >= 0, out, NEG)


def _bias_tiles(rel_table):
    nh = rel_table.shape[1]
    return pl.pallas_call(
        _bias_kernel,
        grid=(nh,),
        in_specs=[pl.BlockSpec(memory_space=pltpu.SMEM)],
        out_specs=pl.BlockSpec((1, 3, TQ, TQ), lambda h: (h, 0, 0, 0)),
        out_shape=jax.ShapeDtypeStruct((nh, 3, TQ, TQ), F32),
        compiler_params=_params(1),
        name="bias_tiles",
    )(rel_table)


def _flash_rows(q, k_ref, v_ref, bias_ref, i, mask_fn):
    scale = HEAD_DIM ** -0.5

    def body(n, carry):
        m, l, acc = carry
        off = pl.multiple_of(n * TQ, TQ)
        kt = k_ref[0, pl.ds(off, TQ), :]
        vt = v_ref[0, pl.ds(off, TQ), :]
        s = lax.dot_general(q, kt, NT_DIMS, preferred_element_type=F32) * scale
        s = s + bias_ref[0, jnp.clip(n - i + 2, 0, 2)] + mask_fn(n)
        m_new = jnp.maximum(m, jnp.max(s, axis=1, keepdims=True))
        alpha = jnp.exp(m - m_new)
        p = jnp.exp(s - m_new)
        l = alpha * l + jnp.sum(p, axis=1, keepdims=True)
        acc = alpha * acc + jnp.dot(p.astype(BF16), vt, preferred_element_type=F32)
        return m_new, l, acc

    init = (jnp.full((TQ, 1), NEG, F32), jnp.zeros((TQ, 1), F32), jnp.zeros((TQ, HEAD_DIM), F32))
    _, l, acc = lax.fori_loop(0, i + 1, body, init)
    return acc / l


def _group_rms_store(y_ref, g_ref, o_ref):
    nh = y_ref.shape[0]
    ss = jnp.zeros((TQ, 1), F32)
    for hh in range(nh):
        y = y_ref[hh]
        ss = ss + jnp.sum(y * y, axis=1, keepdims=True)
    inv = lax.rsqrt(ss / (nh * HEAD_DIM) + LN_EPS)
    for hh in range(nh):
        sl = slice(hh * HEAD_DIM, (hh + 1) * HEAD_DIM)
        o_ref[:, sl] = (y_ref[hh] * inv * g_ref[:, sl]).astype(BF16)


def _moba_kernel(q_ref, k_ref, v_ref, bias_ref, g_ref, o_ref, msel_ref, y_ref):
    i = pl.program_id(1)
    h = pl.program_id(2)
    nb = k_ref.shape[1] // TQ
    q = q_ref[0]

    @pl.when(i <= MOBA_TOPK)
    def _():
        msel_ref[...] = jnp.zeros(msel_ref.shape, F32)

    @pl.when(i > MOBA_TOPK)
    def _():
        gates = []
        for n in range(nb):
            kbar = jnp.mean(k_ref[0, n * TQ:(n + 1) * TQ, :].astype(F32), axis=0, keepdims=True)
            rep = jnp.broadcast_to(kbar, (LANES, HEAD_DIM))
            hi = rep.astype(BF16)
            lo = (rep - hi.astype(F32)).astype(BF16)
            g = (lax.dot_general(q, hi, NT_DIMS, preferred_element_type=F32)
                 + lax.dot_general(q, lo, NT_DIMS, preferred_element_type=F32))
            gates.append(jnp.where(n < i, g, NEG))
        for n in range(nb):
            rank = jnp.zeros((TQ, LANES), F32)
            for mm in range(nb):
                if mm == n:
                    continue
                beats = (gates[mm] >= gates[n]) if mm < n else (gates[mm] > gates[n])
                rank = rank + jnp.where(beats, 1.0, 0.0)
            keep = jnp.logical_or(jnp.logical_and(rank < MOBA_TOPK, n < i), n == i)
            msel_ref[n] = jnp.where(keep, 0.0, NEG)

    def mask_fn(n):
        mk = msel_ref[n]
        return jnp.concatenate([mk] * (TQ // LANES), axis=1)

    y_ref[h] = _flash_rows(q, k_ref, v_ref, bias_ref, i, mask_fn)

    @pl.when(h == pl.num_programs(2) - 1)
    def _():
        _group_rms_store(y_ref, g_ref, o_ref)


def _moba(p_heads, bias, g, batch, seq, nh, q_base, k_base, v_base, bias_base):
    nb = seq // TQ
    return pl.pallas_call(
        _moba_kernel,
        grid=(batch, nb, nh),
        in_specs=[pl.BlockSpec((1, TQ, HEAD_DIM), lambda b, i, h: (q_base + h, b * nb + i, 0)),
                  pl.BlockSpec((1, seq, HEAD_DIM), lambda b, i, h: (k_base + h, b, 0)),
                  pl.BlockSpec((1, seq, HEAD_DIM), lambda b, i, h: (v_base + h, b, 0)),
                  pl.BlockSpec((1, 3, TQ, TQ), lambda b, i, h: (bias_base + h, 0, 0, 0)),
                  pl.BlockSpec((1, nh * HEAD_DIM), lambda b, i, h: (0, 0))],
        out_specs=pl.BlockSpec((TQ, nh * HEAD_DIM), lambda b, i, h: (b * nb + i, 0)),
        out_shape=jax.ShapeDtypeStruct((batch * seq, nh * HEAD_DIM), BF16),
        scratch_shapes=[pltpu.VMEM((nb, TQ, LANES), F32),
                        pltpu.VMEM((nh, TQ, HEAD_DIM), F32)],
        compiler_params=_params(3),
        name="moba_attn",
    )(p_heads, p_heads, p_heads, bias, g)


def _dsa_kernel(q_ref, k_ref, v_ref, bias_ref, qi_ref, ki_ref, wi_ref, g_ref, o_ref,
                key_ref, mask_ref, wb_ref, y_ref):
    i = pl.program_id(1)
    h = pl.program_id(2)
    nb = k_ref.shape[1] // TQ

    @pl.when(h == 0)
    def _build_mask():
        neg_key = _sort_key(jnp.full((TQ, TQ), NEG, F32))
        row = lax.broadcasted_iota(I32, (TQ, TQ), 0)
        col = lax.broadcasted_iota(I32, (TQ, TQ), 1)
        wi = wi_ref[...]
        for hh in range(IDX_HEADS):
            wb_ref[hh] = jnp.broadcast_to(wi[:, hh:hh + 1], (TQ, LANES))

        def score_body(n, _):
            kin = ki_ref[pl.ds(pl.multiple_of(n * TQ, TQ), TQ), :]
            sc = jnp.zeros((TQ, TQ), F32)
            for hh in range(IDX_HEADS):
                s = lax.dot_general(qi_ref[:, hh * IDX_DIM:(hh + 1) * IDX_DIM], kin, NT_DIMS,
                                    preferred_element_type=F32)
                w = wb_ref[hh]
                sc = sc + jnp.maximum(s, 0.0) * jnp.concatenate([w] * (TQ // LANES), axis=1)
            sc = sc + 0.0
            admissible = jnp.logical_or(n < i, col <= row)
            key_ref[n] = jnp.where(admissible, _sort_key(sc), neg_key)
            return 0

        lax.fori_loop(0, i + 1, score_body, 0)

        def count_ge(cand):
            def cnt_body(n, part):
                c = jnp.where(key_ref[n] >= cand, 1.0, 0.0)
                for j in range(TQ // LANES):
                    part = part + c[:, j * LANES:(j + 1) * LANES]
                return part
            part = lax.fori_loop(0, i + 1, cnt_body, jnp.zeros((TQ, LANES), F32))
            return jnp.sum(part, axis=1, keepdims=True)

        def bisect(it, thr):
            cand = thr + jnp.left_shift(jnp.int32(1), 31 - it)
            return jnp.where(count_ge(cand) >= DSA_TOPK, cand, thr)

        thr = lax.fori_loop(0, 32, bisect, jnp.full((TQ, 1), INT_MIN, I32))
        n_ties_wanted = DSA_TOPK - (count_ge(thr + 1))
        upper = jnp.where(row < col, 1.0, 0.0).astype(BF16)

        def mask_body(n, seen):
            kk = key_ref[n]
            eq = kk == thr
            eqf = jnp.where(eq, 1.0, 0.0)
            before = jnp.dot(eqf.astype(BF16), upper, preferred_element_type=F32) + seen
            keep_tie = jnp.where(before < n_ties_wanted, 0.0, NEG)
            mask_ref[n] = jnp.where(kk > thr, 0.0, jnp.where(eq, keep_tie, NEG))
            return seen + jnp.sum(eqf, axis=1, keepdims=True)

        lax.fori_loop(0, i + 1, mask_body, jnp.zeros((TQ, 1), F32))

    y_ref[h] = _flash_rows(q_ref[0], k_ref, v_ref, bias_ref, i, lambda n: mask_ref[n])

    @pl.when(h == pl.num_programs(2) - 1)
    def _():
        _group_rms_store(y_ref, g_ref, o_ref)


def _sort_key(x):
    bits = pltpu.bitcast(x, I32)
    return jnp.where(bits >= 0, bits, bits ^ jnp.int32(0x7FFFFFFF))


def _dsa(p_heads, bias, qi, ki, wi, g, batch, seq, nh, q_base, k_base, v_base, bias_base):
    nb = seq // TQ
    return pl.pallas_call(
        _dsa_kernel,
        grid=(batch, nb, nh),
        in_specs=[pl.BlockSpec((1, TQ, HEAD_DIM), lambda b, i, h: (q_base + h, b * nb + i, 0)),
                  pl.BlockSpec((1, seq, HEAD_DIM), lambda b, i, h: (k_base + h, b, 0)),
                  pl.BlockSpec((1, seq, HEAD_DIM), lambda b, i, h: (v_base + h, b, 0)),
                  pl.BlockSpec((1, 3, TQ, TQ), lambda b, i, h: (bias_base + h, 0, 0, 0)),
                  pl.BlockSpec((TQ, IDX_HEADS * IDX_DIM), lambda b, i, h: (b * nb + i, 0)),
                  pl.BlockSpec((seq, IDX_DIM), lambda b, i, h: (b, 0)),
                  pl.BlockSpec((TQ, IDX_HEADS), lambda b, i, h: (b * nb + i, 0)),
                  pl.BlockSpec((1, nh * HEAD_DIM), lambda b, i, h: (0, 0))],
        out_specs=pl.BlockSpec((TQ, nh * HEAD_DIM), lambda b, i, h: (b * nb + i, 0)),
        out_shape=jax.ShapeDtypeStruct((batch * seq, nh * HEAD_DIM), BF16),
        scratch_shapes=[pltpu.VMEM((nb, TQ, TQ), I32),
                        pltpu.VMEM((nb, TQ, TQ), F32),
                        pltpu.VMEM((IDX_HEADS, TQ, LANES), F32),
                        pltpu.VMEM((nh, TQ, HEAD_DIM), F32)],
        compiler_params=_params(3),
        name="dsa_attn",
    )(p_heads, p_heads, p_heads, bias, qi, ki, wi, g)


def _out_proj_kernel(a_ref, b_ref, wa_ref, wb_ref, x_ref, o_ref, wbf_ref):
    ka = a_ref.shape[1]

    @pl.when(pl.program_id(1) == 0)
    def _():
        wbf_ref[:ka, :] = wa_ref[...].astype(BF16)
        wbf_ref[ka:, :] = wb_ref[...].astype(BF16)

    acc = jnp.dot(a_ref[...], wbf_ref[:ka, :], preferred_element_type=F32)
    acc = acc + jnp.dot(b_ref[...], wbf_ref[ka:, :], preferred_element_type=F32)
    o_ref[...] = ALPHA * x_ref[...] + acc


def _out_proj(ya, yb, w_out, x2d):
    m, ka = ya.shape
    kb = yb.shape[1]
    assert ka == kb
    n = w_out.shape[1]
    tm = _pick(m, (1024, 512, 256))
    tn = _pick(n, (512, 256, 128))
    return pl.pallas_call(
        _out_proj_kernel,
        grid=(n // tn, m // tm),
        in_specs=[pl.BlockSpec((tm, ka), lambda j, i: (i, 0)),
                  pl.BlockSpec((tm, kb), lambda j, i: (i, 0)),
                  pl.BlockSpec((ka, tn), lambda j, i: (0, j)),
                  pl.BlockSpec((kb, tn), lambda j, i: (1, j)),
                  pl.BlockSpec((tm, tn), lambda j, i: (i, j))],
        out_specs=pl.BlockSpec((tm, tn), lambda j, i: (i, j)),
        out_shape=jax.ShapeDtypeStruct((m, n), F32),
        scratch_shapes=[pltpu.VMEM((ka + kb, tn), BF16)],
        compiler_params=_params(2),
        name="out_proj",
    )(ya, yb, w_out, w_out, x2d)


def _ln_kernel(z_ref, g_ref, b_ref, *o_refs):
    z = z_ref[...]
    mu = jnp.mean(z, axis=1, keepdims=True)
    zc = z - mu
    var = jnp.mean(zc * zc, axis=1, keepdims=True)
    y = zc * lax.rsqrt(var + LN_EPS) * g_ref[...] + b_ref[...]
    for o_ref in o_refs:
        o_ref[...] = y.astype(o_ref.dtype)


def _layer_norm(z, g, b, out_dtypes):
    m, d = z.shape
    tm = _pick(m, (256, 128))
    row = pl.BlockSpec((tm, d), lambda i: (i, 0))
    vec = pl.BlockSpec((1, d), lambda i: (0, 0))
    return pl.pallas_call(
        _ln_kernel,
        grid=(m // tm,),
        in_specs=[row, vec, vec],
        out_specs=[row for _ in out_dtypes],
        out_shape=[jax.ShapeDtypeStruct((m, d), dt) for dt in out_dtypes],
        compiler_params=_params(1),
        name="layer_norm",
    )(z, g.reshape(1, d), b.reshape(1, d))


def _causal_conv(u, cw_ref, cb_ref):
    row = lax.broadcasted_iota(I32, u.shape, 0)
    out = None
    for j in range(CONV_WIDTH):
        back = CONV_WIDTH - 1 - j
        uj = u if back == 0 else jnp.where(row >= back, pltpu.roll(u, back, axis=0), 0.0)
        term = cw_ref[j:j + 1, :] * uj
        out = term if out is None else out + term
    return out + cb_ref[...]


def _ffn_up_kernel(x_ref, wv_ref, wg_ref, cwv_ref, cwg_ref, cbv_ref, cbg_ref, o_ref):
    x = x_ref[...]
    uv = jnp.dot(x, wv_ref[...].astype(BF16), preferred_element_type=F32)
    ug = jnp.dot(x, wg_ref[...].astype(BF16), preferred_element_type=F32)
    val = _causal_conv(uv, cwv_ref, cbv_ref)
    gate = _causal_conv(ug, cwg_ref, cbg_ref)
    o_ref[...] = (val * (gate * (1.0 / (1.0 + jnp.exp(-gate))))).astype(BF16)


def _ffn_up(x1_bf, w_up, conv_w, conv_b, batch, seq):
    m, d = x1_bf.shape
    dff = w_up.shape[1] // 2
    tn = _pick(dff, (256, 128))
    nj = dff // tn
    cb = conv_b.reshape(1, 2 * dff)
    return pl.pallas_call(
        _ffn_up_kernel,
        grid=(batch, nj),
        in_specs=[pl.BlockSpec((seq, d), lambda b, j: (b, 0), pipeline_mode=pl.Buffered(1)),
                  pl.BlockSpec((d, tn), lambda b, j: (0, j)),
                  pl.BlockSpec((d, tn), lambda b, j: (0, nj + j)),
                  pl.BlockSpec((CONV_WIDTH, tn), lambda b, j: (0, j)),
                  pl.BlockSpec((CONV_WIDTH, tn), lambda b, j: (0, nj + j)),
                  pl.BlockSpec((1, tn), lambda b, j: (0, j)),
                  pl.BlockSpec((1, tn), lambda b, j: (0, nj + j))],
        out_specs=pl.BlockSpec((seq, tn), lambda b, j: (b, j)),
        out_shape=jax.ShapeDtypeStruct((m, dff), BF16),
        compiler_params=_params(2),
        name="ffn_up",
    )(x1_bf, w_up, w_up, conv_w, conv_w, cb, cb)


def _ffn_down_kernel(h_ref, w_ref, x_ref, o_ref):
    o_ref[...] = ALPHA * x_ref[...] + jnp.dot(h_ref[...], w_ref[...], preferred_element_type=F32)


def _ffn_down(h, w_down_bf, x1):
    m, k = h.shape
    n = w_down_bf.shape[1]
    tm = _pick(m, (512, 256))
    tn = _pick(n, (512, 256, 128))
    return pl.pallas_call(
        _ffn_down_kernel,
        grid=(n // tn, m // tm),
        in_specs=[pl.BlockSpec((tm, k), lambda j, i: (i, 0)),
                  pl.BlockSpec((k, tn), lambda j, i: (0, j)),
                  pl.BlockSpec((tm, tn), lambda j, i: (i, j))],
        out_specs=pl.BlockSpec((tm, tn), lambda j, i: (i, j)),
        out_shape=jax.ShapeDtypeStruct((m, n), F32),
        compiler_params=_params(2),
        name="ffn_down",
    )(h, w_down_bf, x1)


def kernel(x, w_in, rel_table, g_moba, g_dsa, w_out, ln1_g, ln1_b, w_up, conv_w, conv_b, w_down,
           ln2_g, ln2_b):
    batch, seq, d = x.shape
    m = batch * seq
    nh = d // HEAD_DIM // 2
    wg = nh * HEAD_DIM
    n_main = 6 * wg
    n_idx = IDX_HEADS * IDX_DIM + IDX_DIM + IDX_HEADS
    assert w_in.shape == (d, n_main + n_idx) and seq % TQ == 0 and seq // 4 >= DSA_TOPK

    x2d = x.reshape(m, d)
    x_bf = x2d.astype(BF16)
    pad = -n_idx % LANES
    w_idx_bf = jnp.pad(w_in[:, n_main:], ((0, 0), (0, pad))).astype(BF16)

    p_heads = _proj_main(x_bf, w_in, n_main)
    qi, ki, wi = _proj_idx(x_bf, w_idx_bf)
    bias = _bias_tiles(rel_table)

    ya = _moba(p_heads, bias, g_moba.reshape(1, wg), batch, seq, nh, 0, nh, 2 * nh, 0)
    yb = _dsa(p_heads, bias, qi, ki, wi, g_dsa.reshape(1, wg), batch, seq, nh,
              3 * nh, 4 * nh, 5 * nh, nh)

    z1 = _out_proj(ya, yb, w_out, x2d)
    x1, x1_bf = _layer_norm(z1, ln1_g, ln1_b, (F32, BF16))

    hmid = _ffn_up(x1_bf, w_up, conv_w, conv_b, batch, seq)
    z2 = _ffn_down(hmid, w_down.astype(BF16), x1)
    (out,) = _layer_norm(z2, ln2_g, ln2_b, (F32,))
    return out.reshape(batch, seq, d)
```

```python
import functools
import math

import jax
import jax.numpy as jnp
from jax import lax
from jax.experimental import pallas as pl
from jax.experimental.pallas import tpu as pltpu

F32 = jnp.float32
BF16 = jnp.bfloat16
I32 = jnp.int32

HEAD_DIM = 128
MOBA_BLOCK = 256
MOBA_TOPK = 3
IDX_HEADS = 16
IDX_DIM = 64
DSA_TOPK = 256
NUM_BUCKETS = 32
MAX_DISTANCE = 128
CONV_WIDTH = 3
DEPTH = 1
ALPHA = (2.0 * DEPTH) ** 0.25
NEG = -1e30
LN_EPS = 1e-5

TQ = MOBA_BLOCK
LANES = 128
SUBLANES = 8
VMEM_LIMIT = 56 * 1024 * 1024
HEADS_INTERLEAVED = 8
SCORE_LOOKAHEAD = 4
BF16_SUBLANES = 16
VT_ROWS = HEAD_DIM + BF16_SUBLANES

SCALE = HEAD_DIM ** -0.5
INV_SCALE = HEAD_DIM ** 0.5
EXP2_COEF = SCALE * math.log2(math.e)
NT_DIMS = (((1,), (1,)), ((), ()))
INT_MIN = -(2 ** 31)
N_IDX = IDX_HEADS * IDX_DIM + IDX_DIM + IDX_HEADS


def _params(n_axes, vmem=VMEM_LIMIT):
    return pltpu.CompilerParams(dimension_semantics=("arbitrary",) * n_axes,
                                vmem_limit_bytes=vmem)


def _pick(n, candidates):
    for c in candidates:
        if n % c == 0:
            return c
    raise ValueError(f"no tile for {n} among {candidates}")


def _resident(block_shape, index_map):
    return pl.BlockSpec(block_shape, index_map, pipeline_mode=pl.Buffered(1))


def _proj_rows_kernel(x_ref, w_ref, o_ref, wbf_ref):
    @pl.when(pl.program_id(1) == 0)
    def _():
        wbf_ref[...] = w_ref[...].astype(BF16)

    acc = lax.dot_general(x_ref[...], wbf_ref[...], NT_DIMS, preferred_element_type=F32)
    for c in range(o_ref.shape[0]):
        o_ref[c] = acc[:, c * HEAD_DIM:(c + 1) * HEAD_DIM].astype(BF16)


def _proj_cols_kernel(x_ref, w_ref, o_ref, wbf_ref):
    @pl.when(pl.program_id(1) == 0)
    def _():
        wbf_ref[...] = w_ref[...].astype(BF16)

    acc = lax.dot_general(wbf_ref[...], x_ref[...], NT_DIMS, preferred_element_type=F32)
    ones = jnp.ones((VT_ROWS - HEAD_DIM, TQ), BF16)
    for c in range(o_ref.shape[0]):
        for t in range(o_ref.shape[1]):
            o_ref[c, t, :HEAD_DIM, :] = acc[c * HEAD_DIM:(c + 1) * HEAD_DIM, t * TQ:(t + 1) * TQ].astype(BF16)
            o_ref[c, t, HEAD_DIM:, :] = ones


def _proj_qk(x_bf, w_t, wg):
    m, d = x_bf.shape
    tm = _pick(m, (1024, 512, 256))
    tn = _pick(wg, (512, 256, 128))
    tpp = wg // tn
    hp = tn // HEAD_DIM

    def w_map(j, i):
        part = j // tpp
        return ((part + (part >= 2)) * tpp + j % tpp, 0)

    return pl.pallas_call(
        _proj_rows_kernel,
        grid=(4 * tpp, m // tm),
        in_specs=[pl.BlockSpec((tm, d), lambda j, i: (i, 0)),
                  pl.BlockSpec((tn, d), w_map)],
        out_specs=pl.BlockSpec((hp, tm, HEAD_DIM), lambda j, i: (j, i, 0)),
        out_shape=jax.ShapeDtypeStruct((4 * wg // HEAD_DIM, m, HEAD_DIM), BF16),
        scratch_shapes=[pltpu.VMEM((tn, d), BF16)],
        compiler_params=_params(2),
        name="proj_qk",
    )(x_bf, w_t)


def _proj_vt(x_bf, w_t, wg):
    m, d = x_bf.shape
    tm = _pick(m, (1024, 512, 256))
    tn = _pick(wg, (512, 256, 128))
    tpp = wg // tn
    hp = tn // HEAD_DIM
    return pl.pallas_call(
        _proj_cols_kernel,
        grid=(2 * tpp, m // tm),
        in_specs=[pl.BlockSpec((tm, d), lambda j, i: (i, 0)),
                  pl.BlockSpec((tn, d), lambda j, i: ((2 + 3 * (j // tpp)) * tpp + j % tpp, 0))],
        out_specs=pl.BlockSpec((hp, tm // TQ, VT_ROWS, TQ), lambda j, i: (j, i, 0, 0)),
        out_shape=jax.ShapeDtypeStruct((2 * wg // HEAD_DIM, m // TQ, VT_ROWS, TQ), BF16),
        scratch_shapes=[pltpu.VMEM((tn, d), BF16)],
        compiler_params=_params(2),
        name="proj_vt",
    )(x_bf, w_t)


def _proj_idx_kernel(x_ref, wq_ref, wk_ref, ww_ref, qi_ref, ki_ref, wi_ref):
    x = x_ref[...]
    qi = lax.dot_general(x, wq_ref[...].astype(BF16), NT_DIMS, preferred_element_type=F32)
    qi_ref[...] = (qi * IDX_DIM ** -0.5).astype(BF16)
    wk = wk_ref[...].astype(BF16)
    zero = jnp.zeros_like(wk)
    for half, wpad in enumerate((jnp.concatenate([wk, zero], axis=0), jnp.concatenate([zero, wk], axis=0))):
        ki_ref[half] = lax.dot_general(x, wpad, NT_DIMS, preferred_element_type=F32).astype(BF16)
    wi = lax.dot_general(ww_ref[...].astype(BF16), x, NT_DIMS, preferred_element_type=F32)
    wi_ref[...] = wi * IDX_HEADS ** -0.5


def _proj_idx(x_bf, w_t, n_main):
    m, d = x_bf.shape
    tm = _pick(m, (1024, 512, 256))
    nq = IDX_HEADS * IDX_DIM
    assert n_main % nq == 0 and (n_main + nq) % IDX_DIM == 0 and (n_main + nq + IDX_DIM) % IDX_HEADS == 0
    return pl.pallas_call(
        _proj_idx_kernel,
        grid=(m // tm,),
        in_specs=[pl.BlockSpec((tm, d), lambda i: (i, 0)),
                  _resident((nq, d), lambda i: (n_main // nq, 0)),
                  _resident((IDX_DIM, d), lambda i: ((n_main + nq) // IDX_DIM, 0)),
                  _resident((IDX_HEADS, d), lambda i: ((n_main + nq + IDX_DIM) // IDX_HEADS, 0))],
        out_specs=[pl.BlockSpec((tm, nq), lambda i: (i, 0)),
                   pl.BlockSpec((2, tm, 2 * IDX_DIM), lambda i: (0, i, 0)),
                   pl.BlockSpec((IDX_HEADS, tm), lambda i: (0, i))],
        out_shape=[jax.ShapeDtypeStruct((m, nq), BF16),
                   jax.ShapeDtypeStruct((2, m, 2 * IDX_DIM), BF16),
                   jax.ShapeDtypeStruct((IDX_HEADS, m), F32)],
        compiler_params=_params(1),
        name="proj_idx",
    )(x_bf, w_t, w_t, w_t)


def _bias_kernel(tab_ref, o_ref):
    h = pl.program_id(0)
    s = lax.broadcasted_iota(I32, (TQ, TQ), 0)
    t = lax.broadcasted_iota(I32, (TQ, TQ), 1)
    max_exact = NUM_BUCKETS // 2
    for slot in range(2):
        dist = (1 - slot) * TQ + t - s
        d = jnp.maximum(dist, 0)
        df = jnp.maximum(d, max_exact).astype(F32)
        large = max_exact + (jnp.log(df / max_exact) / math.log(MAX_DISTANCE / max_exact)
                             * (NUM_BUCKETS - max_exact)).astype(I32)
        large = jnp.minimum(large, NUM_BUCKETS - 1)
        bucket = jnp.where(d < max_exact, d, large)
        out = jnp.zeros((TQ, TQ), F32)
        for k in range(NUM_BUCKETS):
            out = jnp.where(bucket == k, tab_ref[k, h], out)
        o_ref[0, slot] = jnp.where(dist >= 0, out * INV_SCALE, NEG)


def _bias_tiles(rel_table):
    nh = rel_table.shape[1]
    return pl.pallas_call(
        _bias_kernel,
        grid=(nh,),
        in_specs=[pl.BlockSpec(memory_space=pltpu.SMEM)],
        out_specs=pl.BlockSpec((1, 2, TQ, TQ), lambda h: (h, 0, 0, 0)),
        out_shape=jax.ShapeDtypeStruct((nh, 2, TQ, TQ), F32),
        compiler_params=_params(1),
        name="bias_tiles",
    )(rel_table)


def _softmax_step(state, u, row_term, vt):
    tile_max = jnp.max(u, axis=0, keepdims=True)
    if row_term is not None:
        tile_max = tile_max + row_term
    m_new = tile_max if state is None else jnp.maximum(state[0], tile_max)
    off = m_new if row_term is None else m_new - row_term
    p = jnp.exp2((u - off) * EXP2_COEF)
    pv = jnp.dot(vt, p.astype(BF16), preferred_element_type=F32)
    if state is None:
        return m_new, pv
    m_old, acc_old = state
    alpha = jnp.exp2((m_old - m_new) * EXP2_COEF)
    return m_new, alpha * acc_old + pv


def _pipelined(n_slots, issue, consume):
    pending = {g: issue(g) for g in range(min(SCORE_LOOKAHEAD, n_slots))}
    out = []
    for g in range(n_slots):
        if g + SCORE_LOOKAHEAD < n_slots:
            pending[g + SCORE_LOOKAHEAD] = issue(g + SCORE_LOOKAHEAD)
        out.append(consume(g, pending.pop(g)))
    return tuple(out)


def _attend_heads(i, heads, q_ref, k_ref, vt_ref, bias_ref, far_bias, yt_ref,
                  elem_mask=None, row_mask=None):
    def logits(h, n):
        kt = k_ref[h, pl.ds(pl.multiple_of(n * TQ, TQ), TQ), :]
        u = lax.dot_general(kt, q_ref[h], NT_DIMS, preferred_element_type=F32)
        return u if elem_mask is None else u + elem_mask(n)

    ns = len(heads)

    states = _pipelined(
        ns, lambda g: logits(heads[g], i) + bias_ref[heads[g], 1],
        lambda g, u: _softmax_step(None, u, None, vt_ref[heads[g], i]))

    def adjacent(states):
        def consume(g, u):
            rt = None if row_mask is None else row_mask(g, i - 1)
            return _softmax_step(states[g], u, rt, vt_ref[heads[g], i - 1])
        return _pipelined(ns, lambda g: logits(heads[g], i - 1) + bias_ref[heads[g], 0], consume)

    states = lax.cond(i >= 1, adjacent, lambda s: s, states)

    def far_body(n, states):
        def consume(g, u):
            fb = far_bias(heads[g])
            rt = fb if row_mask is None else fb + row_mask(g, n)
            return _softmax_step(states[g], u, rt, vt_ref[heads[g], n])
        return _pipelined(ns, lambda g: logits(heads[g], n), consume)

    states = lax.fori_loop(0, i - 1, far_body, states)

    ss = jnp.zeros((1, TQ), F32)
    for (m, acc), h in zip(states, heads):
        yt = acc[:HEAD_DIM] / acc[HEAD_DIM:HEAD_DIM + 1]
        yt_ref[h] = yt
        ss = ss + jnp.sum(yt * yt, axis=0, keepdims=True)
    return ss


def _group_rms_store(yt_ref, ss, g_ref, o_ref):
    nh = yt_ref.shape[0]
    inv = lax.rsqrt(ss / (nh * HEAD_DIM) + LN_EPS)
    for h in range(nh):
        sl = slice(h * HEAD_DIM, (h + 1) * HEAD_DIM)
        o_ref[:, sl] = ((yt_ref[h] * inv).T * g_ref[:, sl]).astype(BF16)


def _attn_scratch(nh):
    return [pltpu.VMEM((nh, HEAD_DIM, TQ), F32)]


def _attn_in_specs(nh, seq, q_base, k_base, v_base, bias_base):
    nb = seq // TQ
    assert q_base % nh == 0 and k_base % nh == 0 and v_base % nh == 0 and bias_base % nh == 0
    return [pl.BlockSpec(memory_space=pltpu.SMEM),
            pl.BlockSpec((nh, TQ, HEAD_DIM), lambda b, i: (q_base // nh, b * nb + i, 0)),
            _resident((nh, seq, HEAD_DIM), lambda b, i: (k_base // nh, b, 0)),
            _resident((nh, nb, VT_ROWS, TQ), lambda b, i: (v_base // nh, b, 0, 0)),
            _resident((nh, 2, TQ, TQ), lambda b, i: (bias_base // nh, 0, 0, 0))]


def _moba_kernel(head0, tab_ref, q_ref, k_ref, vt_ref, bias_ref, g_ref, o_ref,
                 kbar_ref, msel_ref, yt_ref):
    i = pl.program_id(1)
    nh = q_ref.shape[0]
    nb = k_ref.shape[1] // TQ
    hg = msel_ref.shape[0]

    @pl.when(i == 0)
    def _():
        def kbar_body(h, carry):
            for n in range(nb):
                blk = k_ref[h, n * TQ:(n + 1) * TQ, :].astype(F32)
                kbar_ref[h, n:n + 1, :] = jnp.mean(blk, axis=0, keepdims=True)
            return carry
        lax.fori_loop(0, nh, kbar_body, 0)

    blk_idx = lax.broadcasted_iota(I32, (nb, TQ), 0)
    past = blk_idx < i

    def group_body(grp, ss):
        heads = [grp * hg + g for g in range(hg)]
        for g, h in enumerate(heads):
            kb = kbar_ref[h]
            hi = kb.astype(BF16)
            lo = (kb - hi.astype(F32)).astype(BF16)
            q = q_ref[h]
            gate = (lax.dot_general(hi, q, NT_DIMS, preferred_element_type=F32)
                    + lax.dot_general(lo, q, NT_DIMS, preferred_element_type=F32))
            gate = jnp.where(past, gate, NEG)
            rank = jnp.zeros((nb, TQ), F32)
            for mm in range(nb):
                row = gate[mm:mm + 1, :]
                rank = rank + jnp.where(blk_idx > mm, jnp.where(row >= gate, 1.0, 0.0),
                                        jnp.where(row > gate, 1.0, 0.0))
            keep = jnp.logical_and(past, rank < MOBA_TOPK)
            msel = jnp.where(keep, 0.0, NEG)
            for n in range(nb):
                msel_ref[g, n] = msel[n:n + 1, :]
        ss_g = _attend_heads(i, heads, q_ref, k_ref, vt_ref, bias_ref,
                             lambda h: tab_ref[NUM_BUCKETS - 1, head0 + h] * INV_SCALE, yt_ref,
                             row_mask=lambda g, n: msel_ref[g, n])
        return ss + ss_g

    ss = lax.fori_loop(0, nh // hg, group_body, jnp.zeros((1, TQ), F32))
    _group_rms_store(yt_ref, ss, g_ref, o_ref)


def _moba(rel_table, qk_heads, vt_tiles, bias, g, batch, seq, nh, q_base, k_base, v_base, bias_base):
    nb = seq // TQ
    hg = min(HEADS_INTERLEAVED, nh)
    return pl.pallas_call(
        functools.partial(_moba_kernel, bias_base),
        grid=(batch, nb),
        in_specs=_attn_in_specs(nh, seq, q_base, k_base, v_base, bias_base)
        + [pl.BlockSpec((1, nh * HEAD_DIM), lambda b, i: (0, 0))],
        out_specs=pl.BlockSpec((TQ, nh * HEAD_DIM), lambda b, i: (b * nb + i, 0)),
        out_shape=jax.ShapeDtypeStruct((batch * seq, nh * HEAD_DIM), BF16),
        scratch_shapes=[pltpu.VMEM((nh, nb, HEAD_DIM), F32),
                        pltpu.VMEM((hg, nb, 1, TQ), F32)] + _attn_scratch(nh),
        compiler_params=_params(2),
        name="moba_attn",
    )(rel_table, qk_heads, qk_heads, vt_tiles, bias, g)


def _sort_key(x):
    bits = pltpu.bitcast(x, I32)
    return jnp.where(bits >= 0, bits, bits ^ jnp.int32(0x7FFFFFFF))


def _dsa_kernel(head0, tab_ref, q_ref, k_ref, vt_ref, bias_ref, qi_ref, ki_ref, wi_ref, g_ref, o_ref,
                key_ref, mask_ref, yt_ref):
    i = pl.program_id(1)
    nh = q_ref.shape[0]
    hg = min(HEADS_INTERLEAVED, nh)
    s_idx = lax.broadcasted_iota(I32, (TQ, TQ), 0)
    t_idx = lax.broadcasted_iota(I32, (TQ, TQ), 1)
    neg_key = _sort_key(jnp.full((TQ, TQ), NEG, F32))

    def score_body(n, carry):
        off = pl.multiple_of(n * TQ, TQ)
        sc = jnp.zeros((TQ, TQ), F32)
        for hh in range(IDX_HEADS):
            kin = ki_ref[hh % 2, pl.ds(off, TQ), :]
            qh = qi_ref[:, (hh // 2) * 2 * IDX_DIM:(hh // 2 + 1) * 2 * IDX_DIM]
            s = lax.dot_general(kin, qh, NT_DIMS, preferred_element_type=F32)
            sc = sc + jnp.maximum(s, 0.0) * wi_ref[hh:hh + 1, :]
        sc = sc + 0.0
        admissible = jnp.logical_or(n < i, s_idx <= t_idx)
        key_ref[n] = jnp.where(admissible, _sort_key(sc), neg_key)
        return carry

    lax.fori_loop(0, i + 1, score_body, 0)

    def count_ge(cand):
        def cnt_body(n, part):
            c = jnp.where(key_ref[n] >= cand, 1.0, 0.0)
            return part + jnp.sum(c.reshape(TQ // SUBLANES, SUBLANES, TQ), axis=0)
        part = lax.fori_loop(0, i + 1, cnt_body, jnp.zeros((SUBLANES, TQ), F32))
        return jnp.sum(part, axis=0, keepdims=True)

    def bisect(it, thr):
        cand = thr + jnp.left_shift(jnp.int32(1), 31 - it)
        return jnp.where(count_ge(cand) >= DSA_TOPK, cand, thr)

    thr = lax.fori_loop(0, 32, bisect, jnp.full((1, TQ), INT_MIN, I32))
    ties_wanted = DSA_TOPK - count_ge(thr + 1)
    lower = jnp.where(t_idx < s_idx, 1.0, 0.0).astype(BF16)

    def mask_body(n, seen):
        kk = key_ref[n]
        eq = kk == thr
        eqf = jnp.where(eq, 1.0, 0.0)
        before = jnp.dot(lower, eqf.astype(BF16), preferred_element_type=F32) + seen
        keep_tie = jnp.where(before < ties_wanted, 0.0, NEG)
        mask_ref[n] = jnp.where(kk > thr, 0.0, jnp.where(eq, keep_tie, NEG))
        return seen + jnp.sum(eqf, axis=0, keepdims=True)

    lax.fori_loop(0, i + 1, mask_body, jnp.zeros((1, TQ), F32))

    def group_body(grp, ss):
        heads = [grp * hg + g for g in range(hg)]
        ss_g = _attend_heads(i, heads, q_ref, k_ref, vt_ref, bias_ref,
                             lambda h: tab_ref[NUM_BUCKETS - 1, head0 + h] * INV_SCALE, yt_ref,
                             elem_mask=lambda n: mask_ref[n])
        return ss + ss_g

    ss = lax.fori_loop(0, nh // hg, group_body, jnp.zeros((1, TQ), F32))
    _group_rms_store(yt_ref, ss, g_ref, o_ref)


def _dsa(rel_table, qk_heads, vt_tiles, bias, qi, ki, wi, g, batch, seq, nh,
         q_base, k_base, v_base, bias_base):
    nb = seq // TQ
    return pl.pallas_call(
        functools.partial(_dsa_kernel, bias_base),
        grid=(batch, nb),
        in_specs=_attn_in_specs(nh, seq, q_base, k_base, v_base, bias_base)
        + [pl.BlockSpec((TQ, IDX_HEADS * IDX_DIM), lambda b, i: (b * nb + i, 0)),
           _resident((2, seq, 2 * IDX_DIM), lambda b, i: (0, b, 0)),
           pl.BlockSpec((IDX_HEADS, TQ), lambda b, i: (0, b * nb + i)),
           pl.BlockSpec((1, nh * HEAD_DIM), lambda b, i: (0, 0))],
        out_specs=pl.BlockSpec((TQ, nh * HEAD_DIM), lambda b, i: (b * nb + i, 0)),
        out_shape=jax.ShapeDtypeStruct((batch * seq, nh * HEAD_DIM), BF16),
        scratch_shapes=[pltpu.VMEM((nb, TQ, TQ), I32),
                        pltpu.VMEM((nb, TQ, TQ), F32)] + _attn_scratch(nh),
        compiler_params=_params(2),
        name="dsa_attn",
    )(rel_table, qk_heads, qk_heads, vt_tiles, bias, qi, ki, wi, g)


def _out_proj_kernel(a_ref, b_ref, wa_ref, wb_ref, x_ref, o_ref, wbf_ref):
    ka = a_ref.shape[1]

    @pl.when(pl.program_id(1) == 0)
    def _():
        wbf_ref[:ka, :] = wa_ref[...].astype(BF16)
        wbf_ref[ka:, :] = wb_ref[...].astype(BF16)

    acc = jnp.dot(a_ref[...], wbf_ref[:ka, :], preferred_element_type=F32)
    acc = acc + jnp.dot(b_ref[...], wbf_ref[ka:, :], preferred_element_type=F32)
    o_ref[...] = ALPHA * x_ref[...] + acc


def _out_proj(ya, yb, w_out, x2d):
    m, ka = ya.shape
    kb = yb.shape[1]
    assert ka == kb
    n = w_out.shape[1]
    tm = _pick(m, (1024, 512, 256))
    tn = _pick(n, (512, 256, 128))
    return pl.pallas_call(
        _out_proj_kernel,
        grid=(n // tn, m // tm),
        in_specs=[pl.BlockSpec((tm, ka), lambda j, i: (i, 0)),
                  pl.BlockSpec((tm, kb), lambda j, i: (i, 0)),
                  pl.BlockSpec((ka, tn), lambda j, i: (0, j)),
                  pl.BlockSpec((kb, tn), lambda j, i: (1, j)),
                  pl.BlockSpec((tm, tn), lambda j, i: (i, j))],
        out_specs=pl.BlockSpec((tm, tn), lambda j, i: (i, j)),
        out_shape=jax.ShapeDtypeStruct((m, n), F32),
        scratch_shapes=[pltpu.VMEM((ka + kb, tn), BF16)],
        compiler_params=_params(2),
        name="out_proj",
    )(ya, yb, w_out, w_out, x2d)


def _ln_kernel(z_ref, g_ref, b_ref, *o_refs):
    z = z_ref[...]
    mu = jnp.mean(z, axis=1, keepdims=True)
    zc = z - mu
    var = jnp.mean(zc * zc, axis=1, keepdims=True)
    y = zc * lax.rsqrt(var + LN_EPS) * g_ref[...] + b_ref[...]
    for o_ref in o_refs:
        o_ref[...] = y.astype(o_ref.dtype)


def _layer_norm(z, g, b, out_dtypes):
    m, d = z.shape
    tm = _pick(m, (256, 128))
    row = pl.BlockSpec((tm, d), lambda i: (i, 0))
    vec = pl.BlockSpec((1, d), lambda i: (0, 0))
    return pl.pallas_call(
        _ln_kernel,
        grid=(m // tm,),
        in_specs=[row, vec, vec],
        out_specs=[row for _ in out_dtypes],
        out_shape=[jax.ShapeDtypeStruct((m, d), dt) for dt in out_dtypes],
        compiler_params=_params(1),
        name="layer_norm",
    )(z, g.reshape(1, d), b.reshape(1, d))


def _causal_conv(u, cw_ref, cb_ref):
    row = lax.broadcasted_iota(I32, u.shape, 0)
    out = None
    for j in range(CONV_WIDTH):
        back = CONV_WIDTH - 1 - j
        uj = u if back == 0 else jnp.where(row >= back, pltpu.roll(u, back, axis=0), 0.0)
        term = cw_ref[j:j + 1, :] * uj
        out = term if out is None else out + term
    return out + cb_ref[...]


def _ffn_up_kernel(x_ref, wv_ref, wg_ref, cwv_ref, cwg_ref, cbv_ref, cbg_ref, o_ref):
    x = x_ref[...]
    uv = jnp.dot(x, wv_ref[...].astype(BF16), preferred_element_type=F32)
    ug = jnp.dot(x, wg_ref[...].astype(BF16), preferred_element_type=F32)
    val = _causal_conv(uv, cwv_ref, cbv_ref)
    gate = _causal_conv(ug, cwg_ref, cbg_ref)
    o_ref[...] = (val * (gate * (1.0 / (1.0 + jnp.exp(-gate))))).astype(BF16)


def _ffn_up(x1_bf, w_up, conv_w, conv_b, batch, seq):
    m, d = x1_bf.shape
    dff = w_up.shape[1] // 2
    tn = _pick(dff, (256, 128))
    nj = dff // tn
    cb = conv_b.reshape(1, 2 * dff)
    return pl.pallas_call(
        _ffn_up_kernel,
        grid=(batch, nj),
        in_specs=[_resident((seq, d), lambda b, j: (b, 0)),
                  pl.BlockSpec((d, tn), lambda b, j: (0, j)),
                  pl.BlockSpec((d, tn), lambda b, j: (0, nj + j)),
                  pl.BlockSpec((CONV_WIDTH, tn), lambda b, j: (0, j)),
                  pl.BlockSpec((CONV_WIDTH, tn), lambda b, j: (0, nj + j)),
                  pl.BlockSpec((1, tn), lambda b, j: (0, j)),
                  pl.BlockSpec((1, tn), lambda b, j: (0, nj + j))],
        out_specs=pl.BlockSpec((seq, tn), lambda b, j: (b, j)),
        out_shape=jax.ShapeDtypeStruct((m, dff), BF16),
        compiler_params=_params(2),
        name="ffn_up",
    )(x1_bf, w_up, w_up, conv_w, conv_w, cb, cb)


def _ffn_down_kernel(h_ref, w_ref, x_ref, o_ref):
    o_ref[...] = ALPHA * x_ref[...] + jnp.dot(h_ref[...], w_ref[...], preferred_element_type=F32)


def _ffn_down(h, w_down_bf, x1):
    m, k = h.shape
    n = w_down_bf.shape[1]
    tm = _pick(m, (512, 256))
    tn = _pick(n, (512, 256, 128))
    return pl.pallas_call(
        _ffn_down_kernel,
        grid=(n // tn, m // tm),
        in_specs=[pl.BlockSpec((tm, k), lambda j, i: (i, 0)),
                  pl.BlockSpec((k, tn), lambda j, i: (0, j)),
                  pl.BlockSpec((tm, tn), lambda j, i: (i, j))],
        out_specs=pl.BlockSpec((tm, tn), lambda j, i: (i, j)),
        out_shape=jax.ShapeDtypeStruct((m, n), F32),
        compiler_params=_params(2),
        name="ffn_down",
    )(h, w_down_bf, x1)


def kernel(x, w_in, rel_table, g_moba, g_dsa, w_out, ln1_g, ln1_b, w_up, conv_w, conv_b, w_down,
           ln2_g, ln2_b):
    batch, seq, d = x.shape
    m = batch * seq
    nh = d // HEAD_DIM // 2
    wg = nh * HEAD_DIM
    n_main = 6 * wg
    assert w_in.shape == (d, n_main + N_IDX) and seq % TQ == 0 and seq // 4 >= DSA_TOPK

    x2d = x.reshape(m, d)
    x_bf = x2d.astype(BF16)
    w_t = w_in.T

    qk_heads = _proj_qk(x_bf, w_t, wg)
    vt_tiles = _proj_vt(x_bf, w_t, wg)
    qi, ki, wi = _proj_idx(x_bf, w_t, n_main)
    bias = _bias_tiles(rel_table)

    ya = _moba(rel_table, qk_heads, vt_tiles, bias, g_moba.reshape(1, wg), batch, seq, nh,
               0, nh, 0, 0)
    yb = _dsa(rel_table, qk_heads, vt_tiles, bias, qi, ki, wi, g_dsa.reshape(1, wg), batch, seq, nh,
              2 * nh, 3 * nh, nh, nh)

    z1 = _out_proj(ya, yb, w_out, x2d)
    x1, x1_bf = _layer_norm(z1, ln1_g, ln1_b, (F32, BF16))

    hmid = _ffn_up(x1_bf, w_up, conv_w, conv_b, batch, seq)
    z2 = _ffn_down(hmid, w_down.astype(BF16), x1)
    (out,) = _layer_norm(z2, ln2_g, ln2_b, (F32,))
    return out.reshape(batch, seq, d)
```

```python
import functools
import math

import jax
import jax.numpy as jnp
from jax import lax
from jax.experimental import pallas as pl
from jax.experimental.pallas import tpu as pltpu

F32 = jnp.float32
BF16 = jnp.bfloat16
I32 = jnp.int32

HEAD_DIM = 128
MOBA_BLOCK = 256
MOBA_TOPK = 3
IDX_HEADS = 16
IDX_DIM = 64
DSA_TOPK = 256
NUM_BUCKETS = 32
MAX_DISTANCE = 128
CONV_WIDTH = 3
DEPTH = 1
ALPHA = (2.0 * DEPTH) ** 0.25
NEG = -1e30
LN_EPS = 1e-5

TQ = MOBA_BLOCK
LANES = 128
SUBLANES = 8
VMEM_LIMIT = 56 * 1024 * 1024
HEADS_INTERLEAVED = 16
SCORE_LOOKAHEAD = 4
BF16_SUBLANES = 16
VT_ROWS = HEAD_DIM + BF16_SUBLANES
FFN_ROW_CHUNK = 512

SCALE = HEAD_DIM ** -0.5
INV_SCALE = HEAD_DIM ** 0.5
EXP2_COEF = SCALE * math.log2(math.e)
NT_DIMS = (((1,), (1,)), ((), ()))
INT_MIN = -(2 ** 31)
N_IDX = IDX_HEADS * IDX_DIM + IDX_DIM + IDX_HEADS


def _params(n_axes, vmem=VMEM_LIMIT):
    return pltpu.CompilerParams(dimension_semantics=("arbitrary",) * n_axes,
                                vmem_limit_bytes=vmem)


def _pick(n, candidates):
    for c in candidates:
        if n % c == 0:
            return c
    raise ValueError(f"no tile for {n} among {candidates}")


def _resident(block_shape, index_map):
    return pl.BlockSpec(block_shape, index_map, pipeline_mode=pl.Buffered(1))


def _proj_rows_kernel(x_ref, w_ref, o_ref, wbf_ref):
    @pl.when(pl.program_id(1) == 0)
    def _():
        wbf_ref[...] = w_ref[...].astype(BF16)

    acc = lax.dot_general(x_ref[...], wbf_ref[...], NT_DIMS, preferred_element_type=F32)
    for c in range(o_ref.shape[0]):
        o_ref[c] = acc[:, c * HEAD_DIM:(c + 1) * HEAD_DIM].astype(BF16)


def _proj_cols_kernel(x_ref, w_ref, o_ref, wbf_ref):
    @pl.when(pl.program_id(1) == 0)
    def _():
        wbf_ref[...] = w_ref[...].astype(BF16)

    acc = lax.dot_general(wbf_ref[...], x_ref[...], NT_DIMS, preferred_element_type=F32)
    ones = jnp.ones((VT_ROWS - HEAD_DIM, TQ), BF16)
    for c in range(o_ref.shape[0]):
        for t in range(o_ref.shape[1]):
            o_ref[c, t, :HEAD_DIM, :] = acc[c * HEAD_DIM:(c + 1) * HEAD_DIM, t * TQ:(t + 1) * TQ].astype(BF16)
            o_ref[c, t, HEAD_DIM:, :] = ones


def _proj_qk(x_bf, w_t, wg):
    m, d = x_bf.shape
    tm = _pick(m, (1024, 512, 256))
    tn = _pick(wg, (512, 256, 128))
    tpp = wg // tn
    hp = tn // HEAD_DIM

    def w_map(j, i):
        part = j // tpp
        return ((part + (part >= 2)) * tpp + j % tpp, 0)

    return pl.pallas_call(
        _proj_rows_kernel,
        grid=(4 * tpp, m // tm),
        in_specs=[pl.BlockSpec((tm, d), lambda j, i: (i, 0)),
                  pl.BlockSpec((tn, d), w_map)],
        out_specs=pl.BlockSpec((hp, tm, HEAD_DIM), lambda j, i: (j, i, 0)),
        out_shape=jax.ShapeDtypeStruct((4 * wg // HEAD_DIM, m, HEAD_DIM), BF16),
        scratch_shapes=[pltpu.VMEM((tn, d), BF16)],
        compiler_params=_params(2),
        name="proj_qk",
    )(x_bf, w_t)


def _proj_vt(x_bf, w_t, wg):
    m, d = x_bf.shape
    tm = _pick(m, (1024, 512, 256))
    tn = _pick(wg, (512, 256, 128))
    tpp = wg // tn
    hp = tn // HEAD_DIM
    return pl.pallas_call(
        _proj_cols_kernel,
        grid=(2 * tpp, m // tm),
        in_specs=[pl.BlockSpec((tm, d), lambda j, i: (i, 0)),
                  pl.BlockSpec((tn, d), lambda j, i: ((2 + 3 * (j // tpp)) * tpp + j % tpp, 0))],
        out_specs=pl.BlockSpec((hp, tm // TQ, VT_ROWS, TQ), lambda j, i: (j, i, 0, 0)),
        out_shape=jax.ShapeDtypeStruct((2 * wg // HEAD_DIM, m // TQ, VT_ROWS, TQ), BF16),
        scratch_shapes=[pltpu.VMEM((tn, d), BF16)],
        compiler_params=_params(2),
        name="proj_vt",
    )(x_bf, w_t)


def _proj_idx_kernel(x_ref, wq_ref, wk_ref, ww_ref, qi_ref, ki_ref, wi_ref):
    x = x_ref[...]
    qi = lax.dot_general(x, wq_ref[...].astype(BF16), NT_DIMS, preferred_element_type=F32)
    qi_ref[...] = (qi * IDX_DIM ** -0.5).astype(BF16)
    wk = wk_ref[...].astype(BF16)
    zero = jnp.zeros_like(wk)
    for half, wpad in enumerate((jnp.concatenate([wk, zero], axis=0), jnp.concatenate([zero, wk], axis=0))):
        ki_ref[half] = lax.dot_general(x, wpad, NT_DIMS, preferred_element_type=F32).astype(BF16)
    wi = lax.dot_general(ww_ref[...].astype(BF16), x, NT_DIMS, preferred_element_type=F32)
    wi_ref[...] = wi * IDX_HEADS ** -0.5


def _proj_idx(x_bf, w_t, n_main):
    m, d = x_bf.shape
    tm = _pick(m, (1024, 512, 256))
    nq = IDX_HEADS * IDX_DIM
    assert n_main % nq == 0 and (n_main + nq) % IDX_DIM == 0 and (n_main + nq + IDX_DIM) % IDX_HEADS == 0
    return pl.pallas_call(
        _proj_idx_kernel,
        grid=(m // tm,),
        in_specs=[pl.BlockSpec((tm, d), lambda i: (i, 0)),
                  _resident((nq, d), lambda i: (n_main // nq, 0)),
                  _resident((IDX_DIM, d), lambda i: ((n_main + nq) // IDX_DIM, 0)),
                  _resident((IDX_HEADS, d), lambda i: ((n_main + nq + IDX_DIM) // IDX_HEADS, 0))],
        out_specs=[pl.BlockSpec((tm, nq), lambda i: (i, 0)),
                   pl.BlockSpec((2, tm, 2 * IDX_DIM), lambda i: (0, i, 0)),
                   pl.BlockSpec((IDX_HEADS, tm), lambda i: (0, i))],
        out_shape=[jax.ShapeDtypeStruct((m, nq), BF16),
                   jax.ShapeDtypeStruct((2, m, 2 * IDX_DIM), BF16),
                   jax.ShapeDtypeStruct((IDX_HEADS, m), F32)],
        compiler_params=_params(1),
        name="proj_idx",
    )(x_bf, w_t, w_t, w_t)


def _bias_kernel(tab_ref, o_ref):
    h = pl.program_id(0)
    s = lax.broadcasted_iota(I32, (TQ, TQ), 0)
    t = lax.broadcasted_iota(I32, (TQ, TQ), 1)
    max_exact = NUM_BUCKETS // 2
    for slot in range(2):
        dist = (1 - slot) * TQ + t - s
        d = jnp.maximum(dist, 0)
        df = jnp.maximum(d, max_exact).astype(F32)
        large = max_exact + (jnp.log(df / max_exact) / math.log(MAX_DISTANCE / max_exact)
                             * (NUM_BUCKETS - max_exact)).astype(I32)
        large = jnp.minimum(large, NUM_BUCKETS - 1)
        bucket = jnp.where(d < max_exact, d, large)
        out = jnp.zeros((TQ, TQ), F32)
        for k in range(NUM_BUCKETS):
            out = jnp.where(bucket == k, tab_ref[k, h], out)
        o_ref[0, slot] = jnp.where(dist >= 0, out * INV_SCALE, NEG)


def _bias_tiles(rel_table):
    nh = rel_table.shape[1]
    return pl.pallas_call(
        _bias_kernel,
        grid=(nh,),
        in_specs=[pl.BlockSpec(memory_space=pltpu.SMEM)],
        out_specs=pl.BlockSpec((1, 2, TQ, TQ), lambda h: (h, 0, 0, 0)),
        out_shape=jax.ShapeDtypeStruct((nh, 2, TQ, TQ), F32),
        compiler_params=_params(1),
        name="bias_tiles",
    )(rel_table)


def _softmax_step(state, u, row_term, vt):
    tile_max = jnp.max(u, axis=0, keepdims=True)
    if row_term is not None:
        tile_max = tile_max + row_term
    m_new = tile_max if state is None else jnp.maximum(state[0], tile_max)
    off = m_new if row_term is None else m_new - row_term
    p = jnp.exp2((u - off) * EXP2_COEF)
    pv = jnp.dot(vt, p.astype(BF16), preferred_element_type=F32)
    if state is None:
        return m_new, pv
    m_old, acc_old = state
    alpha = jnp.exp2((m_old - m_new) * EXP2_COEF)
    return m_new, alpha * acc_old + pv


def _pipelined(n_slots, issue, consume):
    pending = {g: issue(g) for g in range(min(SCORE_LOOKAHEAD, n_slots))}
    out = []
    for g in range(n_slots):
        if g + SCORE_LOOKAHEAD < n_slots:
            pending[g + SCORE_LOOKAHEAD] = issue(g + SCORE_LOOKAHEAD)
        out.append(consume(g, pending.pop(g)))
    return tuple(out)


def _attend_heads(i, heads, q_ref, k_ref, vt_ref, bias_ref, far_bias, yt_ref,
                  elem_mask=None, row_mask=None):
    def logits(h, n):
        kt = k_ref[h, pl.ds(pl.multiple_of(n * TQ, TQ), TQ), :]
        u = lax.dot_general(kt, q_ref[h], NT_DIMS, preferred_element_type=F32)
        return u if elem_mask is None else u + elem_mask(n)

    ns = len(heads)

    states = _pipelined(
        ns, lambda g: logits(heads[g], i) + bias_ref[heads[g], 1],
        lambda g, u: _softmax_step(None, u, None, vt_ref[heads[g], i]))

    def adjacent(states):
        def consume(g, u):
            rt = None if row_mask is None else row_mask(g, i - 1)
            return _softmax_step(states[g], u, rt, vt_ref[heads[g], i - 1])
        return _pipelined(ns, lambda g: logits(heads[g], i - 1) + bias_ref[heads[g], 0], consume)

    states = lax.cond(i >= 1, adjacent, lambda s: s, states)

    def far_body(n, states):
        def consume(g, u):
            fb = far_bias(heads[g])
            rt = fb if row_mask is None else fb + row_mask(g, n)
            return _softmax_step(states[g], u, rt, vt_ref[heads[g], n])
        return _pipelined(ns, lambda g: logits(heads[g], n), consume)

    states = lax.fori_loop(0, i - 1, far_body, states)

    ss = jnp.zeros((1, TQ), F32)
    for (m, acc), h in zip(states, heads):
        yt = acc[:HEAD_DIM] / acc[HEAD_DIM:HEAD_DIM + 1]
        yt_ref[h] = yt
        ss = ss + jnp.sum(yt * yt, axis=0, keepdims=True)
    return ss


def _group_rms_store(yt_ref, ss, g_ref, o_ref):
    nh = yt_ref.shape[0]
    inv = lax.rsqrt(ss / (nh * HEAD_DIM) + LN_EPS)
    for h in range(nh):
        sl = slice(h * HEAD_DIM, (h + 1) * HEAD_DIM)
        o_ref[:, sl] = ((yt_ref[h] * inv).T * g_ref[:, sl]).astype(BF16)


def _attn_scratch(nh):
    return [pltpu.VMEM((nh, HEAD_DIM, TQ), F32)]


def _attn_in_specs(nh, seq, q_base, k_base, v_base, bias_base):
    nb = seq // TQ
    assert q_base % nh == 0 and k_base % nh == 0 and v_base % nh == 0 and bias_base % nh == 0
    return [pl.BlockSpec(memory_space=pltpu.SMEM),
            pl.BlockSpec((nh, TQ, HEAD_DIM), lambda b, i: (q_base // nh, b * nb + i, 0)),
            _resident((nh, seq, HEAD_DIM), lambda b, i: (k_base // nh, b, 0)),
            _resident((nh, nb, VT_ROWS, TQ), lambda b, i: (v_base // nh, b, 0, 0)),
            _resident((nh, 2, TQ, TQ), lambda b, i: (bias_base // nh, 0, 0, 0))]


def _moba_kernel(head0, tab_ref, q_ref, k_ref, vt_ref, bias_ref, g_ref, o_ref,
                 kbar_ref, msel_ref, yt_ref):
    i = pl.program_id(1)
    nh = q_ref.shape[0]
    nb = k_ref.shape[1] // TQ
    hg = msel_ref.shape[0]

    @pl.when(i == 0)
    def _():
        def kbar_body(h, carry):
            for n in range(nb):
                blk = k_ref[h, n * TQ:(n + 1) * TQ, :].astype(F32)
                kbar_ref[h, n:n + 1, :] = jnp.mean(blk, axis=0, keepdims=True)
            return carry
        lax.fori_loop(0, nh, kbar_body, 0)

    blk_idx = lax.broadcasted_iota(I32, (nb, TQ), 0)
    past = blk_idx < i

    def group_body(grp, ss):
        heads = [grp * hg + g for g in range(hg)]
        for g, h in enumerate(heads):
            kb = kbar_ref[h]
            hi = kb.astype(BF16)
            lo = (kb - hi.astype(F32)).astype(BF16)
            q = q_ref[h]
            gate = (lax.dot_general(hi, q, NT_DIMS, preferred_element_type=F32)
                    + lax.dot_general(lo, q, NT_DIMS, preferred_element_type=F32))
            gate = jnp.where(past, gate, NEG)
            rank = jnp.zeros((nb, TQ), F32)
            for mm in range(nb):
                row = gate[mm:mm + 1, :]
                rank = rank + jnp.where(blk_idx > mm, jnp.where(row >= gate, 1.0, 0.0),
                                        jnp.where(row > gate, 1.0, 0.0))
            keep = jnp.logical_and(past, rank < MOBA_TOPK)
            msel = jnp.where(keep, 0.0, NEG)
            for n in range(nb):
                msel_ref[g, n] = msel[n:n + 1, :]
        ss_g = _attend_heads(i, heads, q_ref, k_ref, vt_ref, bias_ref,
                             lambda h: tab_ref[NUM_BUCKETS - 1, head0 + h] * INV_SCALE, yt_ref,
                             row_mask=lambda g, n: msel_ref[g, n])
        return ss + ss_g

    ss = lax.fori_loop(0, nh // hg, group_body, jnp.zeros((1, TQ), F32))
    _group_rms_store(yt_ref, ss, g_ref, o_ref)


def _moba(rel_table, qk_heads, vt_tiles, bias, g, batch, seq, nh, q_base, k_base, v_base, bias_base):
    nb = seq // TQ
    hg = min(HEADS_INTERLEAVED, nh)
    return pl.pallas_call(
        functools.partial(_moba_kernel, bias_base),
        grid=(batch, nb),
        in_specs=_attn_in_specs(nh, seq, q_base, k_base, v_base, bias_base)
        + [pl.BlockSpec((1, nh * HEAD_DIM), lambda b, i: (0, 0))],
        out_specs=pl.BlockSpec((TQ, nh * HEAD_DIM), lambda b, i: (b * nb + i, 0)),
        out_shape=jax.ShapeDtypeStruct((batch * seq, nh * HEAD_DIM), BF16),
        scratch_shapes=[pltpu.VMEM((nh, nb, HEAD_DIM), F32),
                        pltpu.VMEM((hg, nb, 1, TQ), F32)] + _attn_scratch(nh),
        compiler_params=_params(2),
        name="moba_attn",
    )(rel_table, qk_heads, qk_heads, vt_tiles, bias, g)


def _sort_key(x):
    bits = pltpu.bitcast(x, I32)
    return jnp.where(bits >= 0, bits, bits ^ jnp.int32(0x7FFFFFFF))


def _dsa_kernel(head0, tab_ref, q_ref, k_ref, vt_ref, bias_ref, qi_ref, ki_ref, wi_ref, g_ref, o_ref,
                key_ref, mask_ref, yt_ref):
    i = pl.program_id(1)
    nh = q_ref.shape[0]
    hg = min(HEADS_INTERLEAVED, nh)
    s_idx = lax.broadcasted_iota(I32, (TQ, TQ), 0)
    t_idx = lax.broadcasted_iota(I32, (TQ, TQ), 1)
    neg_key = _sort_key(jnp.full((TQ, TQ), NEG, F32))

    def score_body(n, carry):
        off = pl.multiple_of(n * TQ, TQ)
        sc = jnp.zeros((TQ, TQ), F32)
        for hh in range(IDX_HEADS):
            kin = ki_ref[hh % 2, pl.ds(off, TQ), :]
            qh = qi_ref[:, (hh // 2) * 2 * IDX_DIM:(hh // 2 + 1) * 2 * IDX_DIM]
            s = lax.dot_general(kin, qh, NT_DIMS, preferred_element_type=F32)
            sc = sc + jnp.maximum(s, 0.0) * wi_ref[hh:hh + 1, :]
        sc = sc + 0.0
        admissible = jnp.logical_or(n < i, s_idx <= t_idx)
        key_ref[n] = jnp.where(admissible, _sort_key(sc), neg_key)
        return carry

    lax.fori_loop(0, i + 1, score_body, 0)

    def count_ge(cand):
        def cnt_body(n, part):
            c = jnp.where(key_ref[n] >= cand, 1.0, 0.0)
            return part + jnp.sum(c.reshape(TQ // SUBLANES, SUBLANES, TQ), axis=0)
        part = lax.fori_loop(0, i + 1, cnt_body, jnp.zeros((SUBLANES, TQ), F32))
        return jnp.sum(part, axis=0, keepdims=True)

    def bisect(it, thr):
        cand = thr + jnp.left_shift(jnp.int32(1), 31 - it)
        return jnp.where(count_ge(cand) >= DSA_TOPK, cand, thr)

    thr = lax.fori_loop(0, 32, bisect, jnp.full((1, TQ), INT_MIN, I32))
    ties_wanted = DSA_TOPK - count_ge(thr + 1)
    lower = jnp.where(t_idx < s_idx, 1.0, 0.0).astype(BF16)

    def mask_body(n, seen):
        kk = key_ref[n]
        eq = kk == thr
        eqf = jnp.where(eq, 1.0, 0.0)
        before = jnp.dot(lower, eqf.astype(BF16), preferred_element_type=F32) + seen
        keep_tie = jnp.where(before < ties_wanted, 0.0, NEG)
        mask_ref[n] = jnp.where(kk > thr, 0.0, jnp.where(eq, keep_tie, NEG))
        return seen + jnp.sum(eqf, axis=0, keepdims=True)

    lax.fori_loop(0, i + 1, mask_body, jnp.zeros((1, TQ), F32))

    def group_body(grp, ss):
        heads = [grp * hg + g for g in range(hg)]
        ss_g = _attend_heads(i, heads, q_ref, k_ref, vt_ref, bias_ref,
                             lambda h: tab_ref[NUM_BUCKETS - 1, head0 + h] * INV_SCALE, yt_ref,
                             elem_mask=lambda n: mask_ref[n])
        return ss + ss_g

    ss = lax.fori_loop(0, nh // hg, group_body, jnp.zeros((1, TQ), F32))
    _group_rms_store(yt_ref, ss, g_ref, o_ref)


def _dsa(rel_table, qk_heads, vt_tiles, bias, qi, ki, wi, g, batch, seq, nh,
         q_base, k_base, v_base, bias_base):
    nb = seq // TQ
    return pl.pallas_call(
        functools.partial(_dsa_kernel, bias_base),
        grid=(batch, nb),
        in_specs=_attn_in_specs(nh, seq, q_base, k_base, v_base, bias_base)
        + [pl.BlockSpec((TQ, IDX_HEADS * IDX_DIM), lambda b, i: (b * nb + i, 0)),
           _resident((2, seq, 2 * IDX_DIM), lambda b, i: (0, b, 0)),
           pl.BlockSpec((IDX_HEADS, TQ), lambda b, i: (0, b * nb + i)),
           pl.BlockSpec((1, nh * HEAD_DIM), lambda b, i: (0, 0))],
        out_specs=pl.BlockSpec((TQ, nh * HEAD_DIM), lambda b, i: (b * nb + i, 0)),
        out_shape=jax.ShapeDtypeStruct((batch * seq, nh * HEAD_DIM), BF16),
        scratch_shapes=[pltpu.VMEM((nb, TQ, TQ), I32),
                        pltpu.VMEM((nb, TQ, TQ), F32)] + _attn_scratch(nh),
        compiler_params=_params(2),
        name="dsa_attn",
    )(rel_table, qk_heads, qk_heads, vt_tiles, bias, qi, ki, wi, g)


def _out_proj_kernel(a_ref, b_ref, wa_ref, wb_ref, x_ref, o_ref, wbf_ref):
    ka = a_ref.shape[1]

    @pl.when(pl.program_id(1) == 0)
    def _():
        wbf_ref[:ka, :] = wa_ref[...].astype(BF16)
        wbf_ref[ka:, :] = wb_ref[...].astype(BF16)

    acc = jnp.dot(a_ref[...], wbf_ref[:ka, :], preferred_element_type=F32)
    acc = acc + jnp.dot(b_ref[...], wbf_ref[ka:, :], preferred_element_type=F32)
    o_ref[...] = ALPHA * x_ref[...] + acc


def _out_proj(ya, yb, w_out, x2d):
    m, ka = ya.shape
    kb = yb.shape[1]
    assert ka == kb
    n = w_out.shape[1]
    tm = _pick(m, (1024, 512, 256))
    tn = _pick(n, (512, 256, 128))
    return pl.pallas_call(
        _out_proj_kernel,
        grid=(n // tn, m // tm),
        in_specs=[pl.BlockSpec((tm, ka), lambda j, i: (i, 0)),
                  pl.BlockSpec((tm, kb), lambda j, i: (i, 0)),
                  pl.BlockSpec((ka, tn), lambda j, i: (0, j)),
                  pl.BlockSpec((kb, tn), lambda j, i: (1, j)),
                  pl.BlockSpec((tm, tn), lambda j, i: (i, j))],
        out_specs=pl.BlockSpec((tm, tn), lambda j, i: (i, j)),
        out_shape=jax.ShapeDtypeStruct((m, n), F32),
        scratch_shapes=[pltpu.VMEM((ka + kb, tn), BF16)],
        compiler_params=_params(2),
        name="out_proj",
    )(ya, yb, w_out, w_out, x2d)


def _ln_kernel(z_ref, g_ref, b_ref, *o_refs):
    z = z_ref[...]
    mu = jnp.mean(z, axis=1, keepdims=True)
    zc = z - mu
    var = jnp.mean(zc * zc, axis=1, keepdims=True)
    y = zc * lax.rsqrt(var + LN_EPS) * g_ref[...] + b_ref[...]
    for o_ref in o_refs:
        o_ref[...] = y.astype(o_ref.dtype)


def _layer_norm(z, g, b, out_dtypes):
    m, d = z.shape
    tm = _pick(m, (256, 128))
    row = pl.BlockSpec((tm, d), lambda i: (i, 0))
    vec = pl.BlockSpec((1, d), lambda i: (0, 0))
    return pl.pallas_call(
        _ln_kernel,
        grid=(m // tm,),
        in_specs=[row, vec, vec],
        out_specs=[row for _ in out_dtypes],
        out_shape=[jax.ShapeDtypeStruct((m, d), dt) for dt in out_dtypes],
        compiler_params=_params(1),
        name="layer_norm",
    )(z, g.reshape(1, d), b.reshape(1, d))


def _causal_conv(u, history, cw_ref, cb_ref):
    ext = jnp.concatenate([history, u], axis=0)
    out = None
    for j in range(CONV_WIDTH):
        back = CONV_WIDTH - 1 - j
        uj = u if back == 0 else pltpu.roll(ext, back, axis=0)[SUBLANES:]
        term = cw_ref[j:j + 1, :] * uj
        out = term if out is None else out + term
    return out + cb_ref[...]


def _ffn_up_kernel(x_ref, wv_ref, wg_ref, cwv_ref, cwg_ref, cbv_ref, cbg_ref, o_ref):
    wv = wv_ref[...].astype(BF16)
    wg = wg_ref[...].astype(BF16)
    n_chunks = x_ref.shape[0] // FFN_ROW_CHUNK

    def matmuls(c):
        x = x_ref[c * FFN_ROW_CHUNK:(c + 1) * FFN_ROW_CHUNK, :]
        return (jnp.dot(x, wv, preferred_element_type=F32), jnp.dot(x, wg, preferred_element_type=F32))

    pending = matmuls(0)
    history = (jnp.zeros((SUBLANES, wv.shape[1]), F32),) * 2
    for c in range(n_chunks):
        uv, ug = pending
        if c + 1 < n_chunks:
            pending = matmuls(c + 1)
        val = _causal_conv(uv, history[0], cwv_ref, cbv_ref)
        gate = _causal_conv(ug, history[1], cwg_ref, cbg_ref)
        o_ref[c * FFN_ROW_CHUNK:(c + 1) * FFN_ROW_CHUNK, :] = (
            val * (gate * (1.0 / (1.0 + jnp.exp(-gate))))).astype(BF16)
        history = (uv[-SUBLANES:], ug[-SUBLANES:])


def _ffn_up(x1_bf, w_up, conv_w, conv_b, batch, seq):
    m, d = x1_bf.shape
    dff = w_up.shape[1] // 2
    tn = _pick(dff, (256, 128))
    nj = dff // tn
    cb = conv_b.reshape(1, 2 * dff)
    return pl.pallas_call(
        _ffn_up_kernel,
        grid=(batch, nj),
        in_specs=[_resident((seq, d), lambda b, j: (b, 0)),
                  pl.BlockSpec((d, tn), lambda b, j: (0, j)),
                  pl.BlockSpec((d, tn), lambda b, j: (0, nj + j)),
                  pl.BlockSpec((CONV_WIDTH, tn), lambda b, j: (0, j)),
                  pl.BlockSpec((CONV_WIDTH, tn), lambda b, j: (0, nj + j)),
                  pl.BlockSpec((1, tn), lambda b, j: (0, j)),
                  pl.BlockSpec((1, tn), lambda b, j: (0, nj + j))],
        out_specs=pl.BlockSpec((seq, tn), lambda b, j: (b, j)),
        out_shape=jax.ShapeDtypeStruct((m, dff), BF16),
        compiler_params=_params(2),
        name="ffn_up",
    )(x1_bf, w_up, w_up, conv_w, conv_w, cb, cb)


def _ffn_down_kernel(h_ref, w_ref, x_ref, o_ref):
    o_ref[...] = ALPHA * x_ref[...] + jnp.dot(h_ref[...], w_ref[...], preferred_element_type=F32)


def _ffn_down(h, w_down_bf, x1):
    m, k = h.shape
    n = w_down_bf.shape[1]
    tm = _pick(m, (512, 256))
    tn = _pick(n, (512, 256, 128))
    return pl.pallas_call(
        _ffn_down_kernel,
        grid=(n // tn, m // tm),
        in_specs=[pl.BlockSpec((tm, k), lambda j, i: (i, 0)),
                  pl.BlockSpec((k, tn), lambda j, i: (0, j)),
                  pl.BlockSpec((tm, tn), lambda j, i: (i, j))],
        out_specs=pl.BlockSpec((tm, tn), lambda j, i: (i, j)),
        out_shape=jax.ShapeDtypeStruct((m, n), F32),
        compiler_params=_params(2),
        name="ffn_down",
    )(h, w_down_bf, x1)


def kernel(x, w_in, rel_table, g_moba, g_dsa, w_out, ln1_g, ln1_b, w_up, conv_w, conv_b, w_down,
           ln2_g, ln2_b):
    batch, seq, d = x.shape
    m = batch * seq
    nh = d // HEAD_DIM // 2
    wg = nh * HEAD_DIM
    n_main = 6 * wg
    assert w_in.shape == (d, n_main + N_IDX) and seq % TQ == 0 and seq // 4 >= DSA_TOPK

    x2d = x.reshape(m, d)
    x_bf = x2d.astype(BF16)
    w_t = w_in.T

    qk_heads = _proj_qk(x_bf, w_t, wg)
    vt_tiles = _proj_vt(x_bf, w_t, wg)
    qi, ki, wi = _proj_idx(x_bf, w_t, n_main)
    bias = _bias_tiles(rel_table)

    ya = _moba(rel_table, qk_heads, vt_tiles, bias, g_moba.reshape(1, wg), batch, seq, nh,
               0, nh, 0, 0)
    yb = _dsa(rel_table, qk_heads, vt_tiles, bias, qi, ki, wi, g_dsa.reshape(1, wg), batch, seq, nh,
              2 * nh, 3 * nh, nh, nh)

    z1 = _out_proj(ya, yb, w_out, x2d)
    x1, x1_bf = _layer_norm(z1, ln1_g, ln1_b, (F32, BF16))

    hmid = _ffn_up(x1_bf, w_up, conv_w, conv_b, batch, seq)
    z2 = _ffn_down(hmid, w_down.astype(BF16), x1)
    (out,) = _layer_norm(z2, ln2_g, ln2_b, (F32,))
    return out.reshape(batch, seq, d)
```

```python
import functools
import math

import jax
import jax.numpy as jnp
from jax import lax
from jax.experimental import pallas as pl
from jax.experimental.pallas import tpu as pltpu

F32 = jnp.float32
BF16 = jnp.bfloat16
I32 = jnp.int32

HEAD_DIM = 128
MOBA_BLOCK = 256
MOBA_TOPK = 3
IDX_HEADS = 16
IDX_DIM = 64
DSA_TOPK = 256
NUM_BUCKETS = 32
MAX_DISTANCE = 128
CONV_WIDTH = 3
DEPTH = 1
ALPHA = (2.0 * DEPTH) ** 0.25
NEG = -1e30
LN_EPS = 1e-5

TQ = MOBA_BLOCK
LANES = 128
SUBLANES = 8
VMEM_LIMIT = 56 * 1024 * 1024
HEADS_INTERLEAVED = 16
SCORE_LOOKAHEAD = 4
BF16_SUBLANES = 16
VT_ROWS = HEAD_DIM + BF16_SUBLANES
FFN_ROW_CHUNK = 512

SCALE = HEAD_DIM ** -0.5
INV_SCALE = HEAD_DIM ** 0.5
EXP2_COEF = SCALE * math.log2(math.e)
NT_DIMS = (((1,), (1,)), ((), ()))
INT_MIN = -(2 ** 31)
N_IDX = IDX_HEADS * IDX_DIM + IDX_DIM + IDX_HEADS


def _params(n_axes, vmem=VMEM_LIMIT):
    return pltpu.CompilerParams(dimension_semantics=("arbitrary",) * n_axes,
                                vmem_limit_bytes=vmem)


def _pick(n, candidates):
    for c in candidates:
        if n % c == 0:
            return c
    raise ValueError(f"no tile for {n} among {candidates}")


def _resident(block_shape, index_map):
    return pl.BlockSpec(block_shape, index_map, pipeline_mode=pl.Buffered(1))


def _proj_rows_kernel(x_ref, w_ref, o_ref, wbf_ref):
    @pl.when(pl.program_id(1) == 0)
    def _():
        wbf_ref[...] = w_ref[...].astype(BF16)

    acc = lax.dot_general(x_ref[...], wbf_ref[...], NT_DIMS, preferred_element_type=F32)
    for c in range(o_ref.shape[0]):
        o_ref[c] = acc[:, c * HEAD_DIM:(c + 1) * HEAD_DIM].astype(BF16)


def _proj_cols_kernel(x_ref, w_ref, o_ref, wbf_ref):
    @pl.when(pl.program_id(1) == 0)
    def _():
        wbf_ref[...] = w_ref[...].astype(BF16)

    acc = lax.dot_general(wbf_ref[...], x_ref[...], NT_DIMS, preferred_element_type=F32)
    ones = jnp.ones((VT_ROWS - HEAD_DIM, TQ), BF16)
    for c in range(o_ref.shape[0]):
        for t in range(o_ref.shape[1]):
            o_ref[c, t, :HEAD_DIM, :] = acc[c * HEAD_DIM:(c + 1) * HEAD_DIM, t * TQ:(t + 1) * TQ].astype(BF16)
            o_ref[c, t, HEAD_DIM:, :] = ones


def _proj_qk(x_bf, w_t, wg):
    m, d = x_bf.shape
    tm = _pick(m, (1024, 512, 256))
    tn = _pick(wg, (512, 256, 128))
    tpp = wg // tn
    hp = tn // HEAD_DIM

    def w_map(j, i):
        part = j // tpp
        return ((part + (part >= 2)) * tpp + j % tpp, 0)

    return pl.pallas_call(
        _proj_rows_kernel,
        grid=(4 * tpp, m // tm),
        in_specs=[pl.BlockSpec((tm, d), lambda j, i: (i, 0)),
                  pl.BlockSpec((tn, d), w_map)],
        out_specs=pl.BlockSpec((hp, tm, HEAD_DIM), lambda j, i: (j, i, 0)),
        out_shape=jax.ShapeDtypeStruct((4 * wg // HEAD_DIM, m, HEAD_DIM), BF16),
        scratch_shapes=[pltpu.VMEM((tn, d), BF16)],
        compiler_params=_params(2),
        name="proj_qk",
    )(x_bf, w_t)


def _proj_vt(x_bf, w_t, wg):
    m, d = x_bf.shape
    tm = _pick(m, (1024, 512, 256))
    tn = _pick(wg, (512, 256, 128))
    tpp = wg // tn
    hp = tn // HEAD_DIM
    return pl.pallas_call(
        _proj_cols_kernel,
        grid=(2 * tpp, m // tm),
        in_specs=[pl.BlockSpec((tm, d), lambda j, i: (i, 0)),
                  pl.BlockSpec((tn, d), lambda j, i: ((2 + 3 * (j // tpp)) * tpp + j % tpp, 0))],
        out_specs=pl.BlockSpec((hp, tm // TQ, VT_ROWS, TQ), lambda j, i: (j, i, 0, 0)),
        out_shape=jax.ShapeDtypeStruct((2 * wg // HEAD_DIM, m // TQ, VT_ROWS, TQ), BF16),
        scratch_shapes=[pltpu.VMEM((tn, d), BF16)],
        compiler_params=_params(2),
        name="proj_vt",
    )(x_bf, w_t)


def _proj_idx_kernel(x_ref, wq_ref, wk_ref, ww_ref, qi_ref, ki_ref, wi_ref):
    x = x_ref[...]
    qi = lax.dot_general(x, wq_ref[...].astype(BF16), NT_DIMS, preferred_element_type=F32)
    qi_ref[...] = (qi * IDX_DIM ** -0.5).astype(BF16)
    wk = wk_ref[...].astype(BF16)
    ww = ww_ref[...].astype(BF16)
    pad = jnp.zeros((LANES - IDX_DIM - IDX_HEADS, wk.shape[1]), BF16)
    kw = lax.dot_general(x, jnp.concatenate([wk, ww, pad], axis=0), NT_DIMS, preferred_element_type=F32)
    lane = lax.broadcasted_iota(I32, kw.shape, 1)
    k_lo = jnp.where(lane < IDX_DIM, kw, 0.0)
    ki_ref[0] = k_lo.astype(BF16)
    ki_ref[1] = pltpu.roll(k_lo, IDX_DIM, axis=1).astype(BF16)
    wi_ref[...] = kw.T[IDX_DIM:IDX_DIM + IDX_HEADS] * IDX_HEADS ** -0.5


def _proj_idx(x_bf, w_t, n_main):
    m, d = x_bf.shape
    tm = _pick(m, (1024, 512, 256))
    nq = IDX_HEADS * IDX_DIM
    assert n_main % nq == 0 and (n_main + nq) % IDX_DIM == 0 and (n_main + nq + IDX_DIM) % IDX_HEADS == 0
    return pl.pallas_call(
        _proj_idx_kernel,
        grid=(m // tm,),
        in_specs=[pl.BlockSpec((tm, d), lambda i: (i, 0)),
                  _resident((nq, d), lambda i: (n_main // nq, 0)),
                  _resident((IDX_DIM, d), lambda i: ((n_main + nq) // IDX_DIM, 0)),
                  _resident((IDX_HEADS, d), lambda i: ((n_main + nq + IDX_DIM) // IDX_HEADS, 0))],
        out_specs=[pl.BlockSpec((tm, nq), lambda i: (i, 0)),
                   pl.BlockSpec((2, tm, 2 * IDX_DIM), lambda i: (0, i, 0)),
                   pl.BlockSpec((IDX_HEADS, tm), lambda i: (0, i))],
        out_shape=[jax.ShapeDtypeStruct((m, nq), BF16),
                   jax.ShapeDtypeStruct((2, m, 2 * IDX_DIM), BF16),
                   jax.ShapeDtypeStruct((IDX_HEADS, m), F32)],
        compiler_params=_params(1),
        name="proj_idx",
    )(x_bf, w_t, w_t, w_t)


def _bias_kernel(tab_ref, o_ref):
    h = pl.program_id(0)
    s = lax.broadcasted_iota(I32, (TQ, TQ), 0)
    t = lax.broadcasted_iota(I32, (TQ, TQ), 1)
    max_exact = NUM_BUCKETS // 2
    for slot in range(2):
        dist = (1 - slot) * TQ + t - s
        d = jnp.maximum(dist, 0)
        df = jnp.maximum(d, max_exact).astype(F32)
        large = max_exact + (jnp.log(df / max_exact) / math.log(MAX_DISTANCE / max_exact)
                             * (NUM_BUCKETS - max_exact)).astype(I32)
        large = jnp.minimum(large, NUM_BUCKETS - 1)
        bucket = jnp.where(d < max_exact, d, large)
        out = jnp.zeros((TQ, TQ), F32)
        for k in range(NUM_BUCKETS):
            out = jnp.where(bucket == k, tab_ref[k, h], out)
        o_ref[0, slot] = jnp.where(dist >= 0, out * INV_SCALE, NEG)


def _bias_tiles(rel_table):
    nh = rel_table.shape[1]
    return pl.pallas_call(
        _bias_kernel,
        grid=(nh,),
        in_specs=[pl.BlockSpec(memory_space=pltpu.SMEM)],
        out_specs=pl.BlockSpec((1, 2, TQ, TQ), lambda h: (h, 0, 0, 0)),
        out_shape=jax.ShapeDtypeStruct((nh, 2, TQ, TQ), F32),
        compiler_params=_params(1),
        name="bias_tiles",
    )(rel_table)


def _softmax_step(state, u, row_term, vt):
    tile_max = jnp.max(u, axis=0, keepdims=True)
    if row_term is not None:
        tile_max = tile_max + row_term
    m_new = tile_max if state is None else jnp.maximum(state[0], tile_max)
    off = m_new if row_term is None else m_new - row_term
    p = jnp.exp2((u - off) * EXP2_COEF)
    pv = jnp.dot(vt, p.astype(BF16), preferred_element_type=F32)
    if state is None:
        return m_new, pv
    m_old, acc_old = state
    alpha = jnp.exp2((m_old - m_new) * EXP2_COEF)
    return m_new, alpha * acc_old + pv


def _pipelined(n_slots, issue, consume):
    pending = {g: issue(g) for g in range(min(SCORE_LOOKAHEAD, n_slots))}
    out = []
    for g in range(n_slots):
        if g + SCORE_LOOKAHEAD < n_slots:
            pending[g + SCORE_LOOKAHEAD] = issue(g + SCORE_LOOKAHEAD)
        out.append(consume(g, pending.pop(g)))
    return tuple(out)


def _attend_heads(i, heads, q_ref, k_ref, vt_ref, bias_ref, far_bias, yt_ref,
                  elem_mask=None, row_mask=None):
    def logits(h, n):
        kt = k_ref[h, pl.ds(pl.multiple_of(n * TQ, TQ), TQ), :]
        u = lax.dot_general(kt, q_ref[h], NT_DIMS, preferred_element_type=F32)
        return u if elem_mask is None else u + elem_mask(n)

    ns = len(heads)

    states = _pipelined(
        ns, lambda g: logits(heads[g], i) + bias_ref[heads[g], 1],
        lambda g, u: _softmax_step(None, u, None, vt_ref[heads[g], i]))

    def adjacent(states):
        def consume(g, u):
            rt = None if row_mask is None else row_mask(g, i - 1)
            return _softmax_step(states[g], u, rt, vt_ref[heads[g], i - 1])
        return _pipelined(ns, lambda g: logits(heads[g], i - 1) + bias_ref[heads[g], 0], consume)

    states = lax.cond(i >= 1, adjacent, lambda s: s, states)

    def far_body(n, states):
        def consume(g, u):
            fb = far_bias(heads[g])
            rt = fb if row_mask is None else fb + row_mask(g, n)
            return _softmax_step(states[g], u, rt, vt_ref[heads[g], n])
        return _pipelined(ns, lambda g: logits(heads[g], n), consume)

    states = lax.fori_loop(0, i - 1, far_body, states)

    ss = jnp.zeros((1, TQ), F32)
    for (m, acc), h in zip(states, heads):
        yt = acc[:HEAD_DIM] / acc[HEAD_DIM:HEAD_DIM + 1]
        yt_ref[h] = yt
        ss = ss + jnp.sum(yt * yt, axis=0, keepdims=True)
    return ss


def _group_rms_store(yt_ref, ss, g_ref, o_ref):
    nh = yt_ref.shape[0]
    inv = lax.rsqrt(ss / (nh * HEAD_DIM) + LN_EPS)
    for h in range(nh):
        sl = slice(h * HEAD_DIM, (h + 1) * HEAD_DIM)
        o_ref[:, sl] = ((yt_ref[h] * inv).T * g_ref[:, sl]).astype(BF16)


def _attn_scratch(nh):
    return [pltpu.VMEM((nh, HEAD_DIM, TQ), F32)]


def _attn_in_specs(nh, seq, q_base, k_base, v_base, bias_base):
    nb = seq // TQ
    assert q_base % nh == 0 and k_base % nh == 0 and v_base % nh == 0 and bias_base % nh == 0
    return [pl.BlockSpec(memory_space=pltpu.SMEM),
            pl.BlockSpec((nh, TQ, HEAD_DIM), lambda b, i: (q_base // nh, b * nb + i, 0)),
            _resident((nh, seq, HEAD_DIM), lambda b, i: (k_base // nh, b, 0)),
            _resident((nh, nb, VT_ROWS, TQ), lambda b, i: (v_base // nh, b, 0, 0)),
            _resident((nh, 2, TQ, TQ), lambda b, i: (bias_base // nh, 0, 0, 0))]


def _moba_kernel(head0, tab_ref, q_ref, k_ref, vt_ref, bias_ref, g_ref, o_ref,
                 kbar_ref, msel_ref, yt_ref):
    i = pl.program_id(1)
    nh = q_ref.shape[0]
    nb = k_ref.shape[1] // TQ
    hg = msel_ref.shape[0]

    @pl.when(i == 0)
    def _():
        def kbar_body(h, carry):
            for n in range(nb):
                blk = k_ref[h, n * TQ:(n + 1) * TQ, :].astype(F32)
                kbar_ref[h, n:n + 1, :] = jnp.mean(blk, axis=0, keepdims=True)
            return carry
        lax.fori_loop(0, nh, kbar_body, 0)

    blk_idx = lax.broadcasted_iota(I32, (nb, TQ), 0)
    past = blk_idx < i

    def group_body(grp, ss):
        heads = [grp * hg + g for g in range(hg)]
        for g, h in enumerate(heads):
            kb = kbar_ref[h]
            hi = kb.astype(BF16)
            lo = (kb - hi.astype(F32)).astype(BF16)
            both = lax.dot_general(jnp.concatenate([hi, lo], axis=0), q_ref[h], NT_DIMS,
                                   preferred_element_type=F32)
            gate = both[:nb] + both[nb:]
            gate = jnp.where(past, gate, NEG)
            rank = jnp.zeros((nb, TQ), F32)
            for mm in range(nb):
                row = gate[mm:mm + 1, :]
                rank = rank + jnp.where(blk_idx > mm, jnp.where(row >= gate, 1.0, 0.0),
                                        jnp.where(row > gate, 1.0, 0.0))
            keep = jnp.logical_and(past, rank < MOBA_TOPK)
            msel = jnp.where(keep, 0.0, NEG)
            for n in range(nb):
                msel_ref[g, n] = msel[n:n + 1, :]
        ss_g = _attend_heads(i, heads, q_ref, k_ref, vt_ref, bias_ref,
                             lambda h: tab_ref[NUM_BUCKETS - 1, head0 + h] * INV_SCALE, yt_ref,
                             row_mask=lambda g, n: msel_ref[g, n])
        return ss + ss_g

    ss = lax.fori_loop(0, nh // hg, group_body, jnp.zeros((1, TQ), F32))
    _group_rms_store(yt_ref, ss, g_ref, o_ref)


def _moba(rel_table, qk_heads, vt_tiles, bias, g, batch, seq, nh, q_base, k_base, v_base, bias_base):
    nb = seq // TQ
    hg = min(HEADS_INTERLEAVED, nh)
    return pl.pallas_call(
        functools.partial(_moba_kernel, bias_base),
        grid=(batch, nb),
        in_specs=_attn_in_specs(nh, seq, q_base, k_base, v_base, bias_base)
        + [pl.BlockSpec((1, nh * HEAD_DIM), lambda b, i: (0, 0))],
        out_specs=pl.BlockSpec((TQ, nh * HEAD_DIM), lambda b, i: (b * nb + i, 0)),
        out_shape=jax.ShapeDtypeStruct((batch * seq, nh * HEAD_DIM), BF16),
        scratch_shapes=[pltpu.VMEM((nh, nb, HEAD_DIM), F32),
                        pltpu.VMEM((hg, nb, 1, TQ), F32)] + _attn_scratch(nh),
        compiler_params=_params(2),
        name="moba_attn",
    )(rel_table, qk_heads, qk_heads, vt_tiles, bias, g)


def _sort_key(x):
    bits = pltpu.bitcast(x, I32)
    return jnp.where(bits >= 0, bits, bits ^ jnp.int32(0x7FFFFFFF))


def _dsa_kernel(head0, tab_ref, q_ref, k_ref, vt_ref, bias_ref, qi_ref, ki_ref, wi_ref, g_ref, o_ref,
                key_ref, mask_ref, yt_ref):
    i = pl.program_id(1)
    nh = q_ref.shape[0]
    hg = min(HEADS_INTERLEAVED, nh)
    s_idx = lax.broadcasted_iota(I32, (TQ, TQ), 0)
    t_idx = lax.broadcasted_iota(I32, (TQ, TQ), 1)
    neg_key = _sort_key(jnp.full((TQ, TQ), NEG, F32))

    def score_body(n, carry):
        off = pl.multiple_of(n * TQ, TQ)
        sc = jnp.zeros((TQ, TQ), F32)
        for hh in range(IDX_HEADS):
            kin = ki_ref[hh % 2, pl.ds(off, TQ), :]
            qh = qi_ref[:, (hh // 2) * 2 * IDX_DIM:(hh // 2 + 1) * 2 * IDX_DIM]
            s = lax.dot_general(kin, qh, NT_DIMS, preferred_element_type=F32)
            sc = sc + jnp.maximum(s, 0.0) * wi_ref[hh:hh + 1, :]
        sc = sc + 0.0
        admissible = jnp.logical_or(n < i, s_idx <= t_idx)
        key_ref[n] = jnp.where(admissible, _sort_key(sc), neg_key)
        return carry

    lax.fori_loop(0, i + 1, score_body, 0)

    def count_ge(cand):
        def cnt_body(n, part):
            c = jnp.where(key_ref[n] >= cand, 1.0, 0.0)
            return part + jnp.sum(c.reshape(TQ // SUBLANES, SUBLANES, TQ), axis=0)
        part = lax.fori_loop(0, i + 1, cnt_body, jnp.zeros((SUBLANES, TQ), F32))
        return jnp.sum(part, axis=0, keepdims=True)

    def bisect(it, thr):
        cand = thr + jnp.left_shift(jnp.int32(1), 31 - it)
        return jnp.where(count_ge(cand) >= DSA_TOPK, cand, thr)

    thr = lax.fori_loop(0, 32, bisect, jnp.full((1, TQ), INT_MIN, I32))
    ties_wanted = DSA_TOPK - count_ge(thr + 1)
    lower = jnp.where(t_idx < s_idx, 1.0, 0.0).astype(BF16)

    def mask_body(n, seen):
        kk = key_ref[n]
        eq = kk == thr
        eqf = jnp.where(eq, 1.0, 0.0)
        before = jnp.dot(lower, eqf.astype(BF16), preferred_element_type=F32) + seen
        keep_tie = jnp.where(before < ties_wanted, 0.0, NEG)
        mask_ref[n] = jnp.where(kk > thr, 0.0, jnp.where(eq, keep_tie, NEG))
        return seen + jnp.sum(eqf, axis=0, keepdims=True)

    lax.fori_loop(0, i + 1, mask_body, jnp.zeros((1, TQ), F32))

    def group_body(grp, ss):
        heads = [grp * hg + g for g in range(hg)]
        ss_g = _attend_heads(i, heads, q_ref, k_ref, vt_ref, bias_ref,
                             lambda h: tab_ref[NUM_BUCKETS - 1, head0 + h] * INV_SCALE, yt_ref,
                             elem_mask=lambda n: mask_ref[n])
        return ss + ss_g

    ss = lax.fori_loop(0, nh // hg, group_body, jnp.zeros((1, TQ), F32))
    _group_rms_store(yt_ref, ss, g_ref, o_ref)


def _dsa(rel_table, qk_heads, vt_tiles, bias, qi, ki, wi, g, batch, seq, nh,
         q_base, k_base, v_base, bias_base):
    nb = seq // TQ
    return pl.pallas_call(
        functools.partial(_dsa_kernel, bias_base),
        grid=(batch, nb),
        in_specs=_attn_in_specs(nh, seq, q_base, k_base, v_base, bias_base)
        + [pl.BlockSpec((TQ, IDX_HEADS * IDX_DIM), lambda b, i: (b * nb + i, 0)),
           _resident((2, seq, 2 * IDX_DIM), lambda b, i: (0, b, 0)),
           pl.BlockSpec((IDX_HEADS, TQ), lambda b, i: (0, b * nb + i)),
           pl.BlockSpec((1, nh * HEAD_DIM), lambda b, i: (0, 0))],
        out_specs=pl.BlockSpec((TQ, nh * HEAD_DIM), lambda b, i: (b * nb + i, 0)),
        out_shape=jax.ShapeDtypeStruct((batch * seq, nh * HEAD_DIM), BF16),
        scratch_shapes=[pltpu.VMEM((nb, TQ, TQ), I32),
                        pltpu.VMEM((nb, TQ, TQ), F32)] + _attn_scratch(nh),
        compiler_params=_params(2),
        name="dsa_attn",
    )(rel_table, qk_heads, qk_heads, vt_tiles, bias, qi, ki, wi, g)


def _out_proj_kernel(a_ref, b_ref, wa_ref, wb_ref, x_ref, o_ref, wbf_ref):
    ka = a_ref.shape[1]

    @pl.when(pl.program_id(1) == 0)
    def _():
        wbf_ref[:ka, :] = wa_ref[...].astype(BF16)
        wbf_ref[ka:, :] = wb_ref[...].astype(BF16)

    acc = jnp.dot(a_ref[...], wbf_ref[:ka, :], preferred_element_type=F32)
    acc = acc + jnp.dot(b_ref[...], wbf_ref[ka:, :], preferred_element_type=F32)
    o_ref[...] = ALPHA * x_ref[...] + acc


def _out_proj(ya, yb, w_out, x2d):
    m, ka = ya.shape
    kb = yb.shape[1]
    assert ka == kb
    n = w_out.shape[1]
    tm = _pick(m, (1024, 512, 256))
    tn = _pick(n, (512, 256, 128))
    return pl.pallas_call(
        _out_proj_kernel,
        grid=(n // tn, m // tm),
        in_specs=[pl.BlockSpec((tm, ka), lambda j, i: (i, 0)),
                  pl.BlockSpec((tm, kb), lambda j, i: (i, 0)),
                  pl.BlockSpec((ka, tn), lambda j, i: (0, j)),
                  pl.BlockSpec((kb, tn), lambda j, i: (1, j)),
                  pl.BlockSpec((tm, tn), lambda j, i: (i, j))],
        out_specs=pl.BlockSpec((tm, tn), lambda j, i: (i, j)),
        out_shape=jax.ShapeDtypeStruct((m, n), F32),
        scratch_shapes=[pltpu.VMEM((ka + kb, tn), BF16)],
        compiler_params=_params(2),
        name="out_proj",
    )(ya, yb, w_out, w_out, x2d)


def _ln_kernel(z_ref, g_ref, b_ref, *o_refs):
    z = z_ref[...]
    mu = jnp.mean(z, axis=1, keepdims=True)
    zc = z - mu
    var = jnp.mean(zc * zc, axis=1, keepdims=True)
    y = zc * lax.rsqrt(var + LN_EPS) * g_ref[...] + b_ref[...]
    for o_ref in o_refs:
        o_ref[...] = y.astype(o_ref.dtype)


def _layer_norm(z, g, b, out_dtypes):
    m, d = z.shape
    tm = _pick(m, (256, 128))
    row = pl.BlockSpec((tm, d), lambda i: (i, 0))
    vec = pl.BlockSpec((1, d), lambda i: (0, 0))
    return pl.pallas_call(
        _ln_kernel,
        grid=(m // tm,),
        in_specs=[row, vec, vec],
        out_specs=[row for _ in out_dtypes],
        out_shape=[jax.ShapeDtypeStruct((m, d), dt) for dt in out_dtypes],
        compiler_params=_params(1),
        name="layer_norm",
    )(z, g.reshape(1, d), b.reshape(1, d))


def _causal_conv(u, history, cw_ref, cb_ref):
    ext = jnp.concatenate([history, u], axis=0)
    out = None
    for j in range(CONV_WIDTH):
        back = CONV_WIDTH - 1 - j
        uj = u if back == 0 else pltpu.roll(ext, back, axis=0)[SUBLANES:]
        term = cw_ref[j:j + 1, :] * uj
        out = term if out is None else out + term
    return out + cb_ref[...]


def _ffn_up_kernel(x_ref, wv_ref, wg_ref, cwv_ref, cwg_ref, cbv_ref, cbg_ref, o_ref):
    wv = wv_ref[...].astype(BF16)
    wg = wg_ref[...].astype(BF16)
    n_chunks = x_ref.shape[0] // FFN_ROW_CHUNK

    def matmuls(c):
        x = x_ref[c * FFN_ROW_CHUNK:(c + 1) * FFN_ROW_CHUNK, :]
        return (jnp.dot(x, wv, preferred_element_type=F32), jnp.dot(x, wg, preferred_element_type=F32))

    pending = matmuls(0)
    history = (jnp.zeros((SUBLANES, wv.shape[1]), F32),) * 2
    for c in range(n_chunks):
        uv, ug = pending
        if c + 1 < n_chunks:
            pending = matmuls(c + 1)
        val = _causal_conv(uv, history[0], cwv_ref, cbv_ref)
        gate = _causal_conv(ug, history[1], cwg_ref, cbg_ref)
        o_ref[c * FFN_ROW_CHUNK:(c + 1) * FFN_ROW_CHUNK, :] = (
            val * (gate * (1.0 / (1.0 + jnp.exp(-gate))))).astype(BF16)
        history = (uv[-SUBLANES:], ug[-SUBLANES:])


def _ffn_up(x1_bf, w_up, conv_w, conv_b, batch, seq):
    m, d = x1_bf.shape
    dff = w_up.shape[1] // 2
    tn = _pick(dff, (256, 128))
    nj = dff // tn
    cb = conv_b.reshape(1, 2 * dff)
    return pl.pallas_call(
        _ffn_up_kernel,
        grid=(batch, nj),
        in_specs=[_resident((seq, d), lambda b, j: (b, 0)),
                  pl.BlockSpec((d, tn), lambda b, j: (0, j)),
                  pl.BlockSpec((d, tn), lambda b, j: (0, nj + j)),
                  pl.BlockSpec((CONV_WIDTH, tn), lambda b, j: (0, j)),
                  pl.BlockSpec((CONV_WIDTH, tn), lambda b, j: (0, nj + j)),
                  pl.BlockSpec((1, tn), lambda b, j: (0, j)),
                  pl.BlockSpec((1, tn), lambda b, j: (0, nj + j))],
        out_specs=pl.BlockSpec((seq, tn), lambda b, j: (b, j)),
        out_shape=jax.ShapeDtypeStruct((m, dff), BF16),
        compiler_params=_params(2),
        name="ffn_up",
    )(x1_bf, w_up, w_up, conv_w, conv_w, cb, cb)


def _ffn_down_kernel(h_ref, w_ref, x_ref, o_ref):
    o_ref[...] = ALPHA * x_ref[...] + jnp.dot(h_ref[...], w_ref[...], preferred_element_type=F32)


def _ffn_down(h, w_down_bf, x1):
    m, k = h.shape
    n = w_down_bf.shape[1]
    tm = _pick(m, (512, 256))
    tn = _pick(n, (512, 256, 128))
    return pl.pallas_call(
        _ffn_down_kernel,
        grid=(n // tn, m // tm),
        in_specs=[pl.BlockSpec((tm, k), lambda j, i: (i, 0)),
                  pl.BlockSpec((k, tn), lambda j, i: (0, j)),
                  pl.BlockSpec((tm, tn), lambda j, i: (i, j))],
        out_specs=pl.BlockSpec((tm, tn), lambda j, i: (i, j)),
        out_shape=jax.ShapeDtypeStruct((m, n), F32),
        compiler_params=_params(2),
        name="ffn_down",
    )(h, w_down_bf, x1)


def kernel(x, w_in, rel_table, g_moba, g_dsa, w_out, ln1_g, ln1_b, w_up, conv_w, conv_b, w_down,
           ln2_g, ln2_b):
    batch, seq, d = x.shape
    m = batch * seq
    nh = d // HEAD_DIM // 2
    wg = nh * HEAD_DIM
    n_main = 6 * wg
    assert w_in.shape == (d, n_main + N_IDX) and seq % TQ == 0 and seq // 4 >= DSA_TOPK

    x2d = x.reshape(m, d)
    x_bf = x2d.astype(BF16)
    w_t = w_in.T

    qk_heads = _proj_qk(x_bf, w_t, wg)
    vt_tiles = _proj_vt(x_bf, w_t, wg)
    qi, ki, wi = _proj_idx(x_bf, w_t, n_main)
    bias = _bias_tiles(rel_table)

    ya = _moba(rel_table, qk_heads, vt_tiles, bias, g_moba.reshape(1, wg), batch, seq, nh,
               0, nh, 0, 0)
    yb = _dsa(rel_table, qk_heads, vt_tiles, bias, qi, ki, wi, g_dsa.reshape(1, wg), batch, seq, nh,
              2 * nh, 3 * nh, nh, nh)

    z1 = _out_proj(ya, yb, w_out, x2d)
    x1, x1_bf = _layer_norm(z1, ln1_g, ln1_b, (F32, BF16))

    hmid = _ffn_up(x1_bf, w_up, conv_w, conv_b, batch, seq)
    z2 = _ffn_down(hmid, w_down.astype(BF16), x1)
    (out,) = _layer_norm(z2, ln2_g, ln2_b, (F32,))
    return out.reshape(batch, seq, d)
```

```python
import functools
import math

import jax
import jax.numpy as jnp
from jax import lax
from jax.experimental import pallas as pl
from jax.experimental.pallas import tpu as pltpu

F32 = jnp.float32
BF16 = jnp.bfloat16
I32 = jnp.int32

HEAD_DIM = 128
MOBA_BLOCK = 256
MOBA_TOPK = 3
IDX_HEADS = 16
IDX_DIM = 64
DSA_TOPK = 256
NUM_BUCKETS = 32
MAX_DISTANCE = 128
CONV_WIDTH = 3
DEPTH = 1
ALPHA = (2.0 * DEPTH) ** 0.25
NEG = -1e30
LN_EPS = 1e-5

TQ = MOBA_BLOCK
LANES = 128
SUBLANES = 8
VMEM_LIMIT = 56 * 1024 * 1024
HEADS_INTERLEAVED = 16
SCORE_LOOKAHEAD = 4
BF16_SUBLANES = 16
VT_ROWS = HEAD_DIM + BF16_SUBLANES
FFN_ROW_CHUNK = 512

SCALE = HEAD_DIM ** -0.5
INV_SCALE = HEAD_DIM ** 0.5
EXP2_COEF = SCALE * math.log2(math.e)
NT_DIMS = (((1,), (1,)), ((), ()))
INT_MIN = -(2 ** 31)
N_IDX = IDX_HEADS * IDX_DIM + IDX_DIM + IDX_HEADS


def _params(n_axes, vmem=VMEM_LIMIT):
    return pltpu.CompilerParams(dimension_semantics=("arbitrary",) * n_axes,
                                vmem_limit_bytes=vmem)


def _pick(n, candidates):
    for c in candidates:
        if n % c == 0:
            return c
    raise ValueError(f"no tile for {n} among {candidates}")


def _resident(block_shape, index_map):
    return pl.BlockSpec(block_shape, index_map, pipeline_mode=pl.Buffered(1))


def _proj_rows_kernel(x_ref, w_ref, o_ref, wbf_ref):
    @pl.when(pl.program_id(1) == 0)
    def _():
        wbf_ref[...] = w_ref[...].astype(BF16)

    acc = lax.dot_general(x_ref[...], wbf_ref[...], NT_DIMS, preferred_element_type=F32)
    for c in range(o_ref.shape[0]):
        o_ref[c] = acc[:, c * HEAD_DIM:(c + 1) * HEAD_DIM].astype(BF16)


def _proj_cols_kernel(x_ref, w_ref, o_ref, wbf_ref):
    @pl.when(pl.program_id(1) == 0)
    def _():
        wbf_ref[...] = w_ref[...].astype(BF16)

    acc = lax.dot_general(wbf_ref[...], x_ref[...], NT_DIMS, preferred_element_type=F32)
    ones = jnp.ones((VT_ROWS - HEAD_DIM, TQ), BF16)
    for c in range(o_ref.shape[0]):
        for t in range(o_ref.shape[1]):
            o_ref[c, t, :HEAD_DIM, :] = acc[c * HEAD_DIM:(c + 1) * HEAD_DIM, t * TQ:(t + 1) * TQ].astype(BF16)
            o_ref[c, t, HEAD_DIM:, :] = ones


def _proj_qk(x_bf, w_t, wg):
    m, d = x_bf.shape
    tm = _pick(m, (1024, 512, 256))
    tn = _pick(wg, (512, 256, 128))
    tpp = wg // tn
    hp = tn // HEAD_DIM

    def w_map(j, i):
        part = j // tpp
        return ((part + (part >= 2)) * tpp + j % tpp, 0)

    return pl.pallas_call(
        _proj_rows_kernel,
        grid=(4 * tpp, m // tm),
        in_specs=[pl.BlockSpec((tm, d), lambda j, i: (i, 0)),
                  pl.BlockSpec((tn, d), w_map)],
        out_specs=pl.BlockSpec((hp, tm, HEAD_DIM), lambda j, i: (j, i, 0)),
        out_shape=jax.ShapeDtypeStruct((4 * wg // HEAD_DIM, m, HEAD_DIM), BF16),
        scratch_shapes=[pltpu.VMEM((tn, d), BF16)],
        compiler_params=_params(2),
        name="proj_qk",
    )(x_bf, w_t)


def _proj_vt(x_bf, w_t, wg):
    m, d = x_bf.shape
    tm = _pick(m, (1024, 512, 256))
    tn = _pick(wg, (512, 256, 128))
    tpp = wg // tn
    hp = tn // HEAD_DIM
    return pl.pallas_call(
        _proj_cols_kernel,
        grid=(2 * tpp, m // tm),
        in_specs=[pl.BlockSpec((tm, d), lambda j, i: (i, 0)),
                  pl.BlockSpec((tn, d), lambda j, i: ((2 + 3 * (j // tpp)) * tpp + j % tpp, 0))],
        out_specs=pl.BlockSpec((hp, tm // TQ, VT_ROWS, TQ), lambda j, i: (j, i, 0, 0)),
        out_shape=jax.ShapeDtypeStruct((2 * wg // HEAD_DIM, m // TQ, VT_ROWS, TQ), BF16),
        scratch_shapes=[pltpu.VMEM((tn, d), BF16)],
        compiler_params=_params(2),
        name="proj_vt",
    )(x_bf, w_t)


def _proj_idx_kernel(x_ref, wq_ref, wk_ref, ww_ref, qi_ref, ki_ref, wi_ref):
    x = x_ref[...]
    qi = lax.dot_general(x, wq_ref[...].astype(BF16), NT_DIMS, preferred_element_type=F32)
    qi_ref[...] = (qi * IDX_DIM ** -0.5).astype(BF16)
    wk = wk_ref[...].astype(BF16)
    ww = ww_ref[...].astype(BF16)
    pad = jnp.zeros((LANES - IDX_DIM - IDX_HEADS, wk.shape[1]), BF16)
    kw = lax.dot_general(x, jnp.concatenate([wk, ww, pad], axis=0), NT_DIMS, preferred_element_type=F32)
    lane = lax.broadcasted_iota(I32, kw.shape, 1)
    k_lo = jnp.where(lane < IDX_DIM, kw, 0.0)
    ki_ref[0] = k_lo.astype(BF16)
    ki_ref[1] = pltpu.roll(k_lo, IDX_DIM, axis=1).astype(BF16)
    wi_ref[...] = kw.T[IDX_DIM:IDX_DIM + IDX_HEADS] * IDX_HEADS ** -0.5


def _proj_idx(x_bf, w_t, n_main):
    m, d = x_bf.shape
    tm = _pick(m, (1024, 512, 256))
    nq = IDX_HEADS * IDX_DIM
    assert n_main % nq == 0 and (n_main + nq) % IDX_DIM == 0 and (n_main + nq + IDX_DIM) % IDX_HEADS == 0
    return pl.pallas_call(
        _proj_idx_kernel,
        grid=(m // tm,),
        in_specs=[pl.BlockSpec((tm, d), lambda i: (i, 0)),
                  _resident((nq, d), lambda i: (n_main // nq, 0)),
                  _resident((IDX_DIM, d), lambda i: ((n_main + nq) // IDX_DIM, 0)),
                  _resident((IDX_HEADS, d), lambda i: ((n_main + nq + IDX_DIM) // IDX_HEADS, 0))],
        out_specs=[pl.BlockSpec((tm, nq), lambda i: (i, 0)),
                   pl.BlockSpec((2, tm, 2 * IDX_DIM), lambda i: (0, i, 0)),
                   pl.BlockSpec((IDX_HEADS, tm), lambda i: (0, i))],
        out_shape=[jax.ShapeDtypeStruct((m, nq), BF16),
                   jax.ShapeDtypeStruct((2, m, 2 * IDX_DIM), BF16),
                   jax.ShapeDtypeStruct((IDX_HEADS, m), F32)],
        compiler_params=_params(1),
        name="proj_idx",
    )(x_bf, w_t, w_t, w_t)


def _bias_kernel(tab_ref, o_ref):
    h = pl.program_id(0)
    s = lax.broadcasted_iota(I32, (TQ, TQ), 0)
    t = lax.broadcasted_iota(I32, (TQ, TQ), 1)
    max_exact = NUM_BUCKETS // 2
    for slot in range(2):
        dist = (1 - slot) * TQ + t - s
        d = jnp.maximum(dist, 0)
        df = jnp.maximum(d, max_exact).astype(F32)
        large = max_exact + (jnp.log(df / max_exact) / math.log(MAX_DISTANCE / max_exact)
                             * (NUM_BUCKETS - max_exact)).astype(I32)
        large = jnp.minimum(large, NUM_BUCKETS - 1)
        bucket = jnp.where(d < max_exact, d, large)
        out = jnp.zeros((TQ, TQ), F32)
        for k in range(NUM_BUCKETS):
            out = jnp.where(bucket == k, tab_ref[k, h], out)
        o_ref[0, slot] = jnp.where(dist >= 0, out * INV_SCALE, NEG)


def _bias_tiles(rel_table):
    nh = rel_table.shape[1]
    return pl.pallas_call(
        _bias_kernel,
        grid=(nh,),
        in_specs=[pl.BlockSpec(memory_space=pltpu.SMEM)],
        out_specs=pl.BlockSpec((1, 2, TQ, TQ), lambda h: (h, 0, 0, 0)),
        out_shape=jax.ShapeDtypeStruct((nh, 2, TQ, TQ), F32),
        compiler_params=_params(1),
        name="bias_tiles",
    )(rel_table)


def _softmax_step(state, u, row_term, vt):
    tile_max = jnp.max(u, axis=0, keepdims=True)
    if row_term is not None:
        tile_max = tile_max + row_term
    m_new = tile_max if state is None else jnp.maximum(state[0], tile_max)
    off = m_new if row_term is None else m_new - row_term
    p = jnp.exp2((u - off) * EXP2_COEF)
    pv = jnp.dot(vt, p.astype(BF16), preferred_element_type=F32)
    if state is None:
        return m_new, pv
    m_old, acc_old = state
    alpha = jnp.exp2((m_old - m_new) * EXP2_COEF)
    return m_new, alpha * acc_old + pv


def _pipelined(n_slots, issue, consume):
    pending = {g: issue(g) for g in range(min(SCORE_LOOKAHEAD, n_slots))}
    out = []
    for g in range(n_slots):
        if g + SCORE_LOOKAHEAD < n_slots:
            pending[g + SCORE_LOOKAHEAD] = issue(g + SCORE_LOOKAHEAD)
        out.append(consume(g, pending.pop(g)))
    return tuple(out)


def _attend_heads(i, heads, q_ref, k_ref, vt_ref, bias_ref, far_bias, yt_ref,
                  elem_mask=None, row_mask=None):
    def logits(h, n):
        kt = k_ref[h, pl.ds(pl.multiple_of(n * TQ, TQ), TQ), :]
        u = lax.dot_general(kt, q_ref[h], NT_DIMS, preferred_element_type=F32)
        return u if elem_mask is None else u + elem_mask(n)

    ns = len(heads)

    states = _pipelined(
        ns, lambda g: logits(heads[g], i) + bias_ref[heads[g], 1],
        lambda g, u: _softmax_step(None, u, None, vt_ref[heads[g], i]))

    def adjacent(states):
        def consume(g, u):
            rt = None if row_mask is None else row_mask(g, i - 1)
            return _softmax_step(states[g], u, rt, vt_ref[heads[g], i - 1])
        return _pipelined(ns, lambda g: logits(heads[g], i - 1) + bias_ref[heads[g], 0], consume)

    states = lax.cond(i >= 1, adjacent, lambda s: s, states)

    def far_body(n, states):
        def consume(g, u):
            fb = far_bias(heads[g])
            rt = fb if row_mask is None else fb + row_mask(g, n)
            return _softmax_step(states[g], u, rt, vt_ref[heads[g], n])
        return _pipelined(ns, lambda g: logits(heads[g], n), consume)

    states = lax.fori_loop(0, i - 1, far_body, states)

    ss = jnp.zeros((1, TQ), F32)
    for (m, acc), h in zip(states, heads):
        yt = acc[:HEAD_DIM] / acc[HEAD_DIM:HEAD_DIM + 1]
        yt_ref[h] = yt
        ss = ss + jnp.sum(yt * yt, axis=0, keepdims=True)
    return ss


def _group_rms_store(yt_ref, ss, g_ref, o_ref):
    nh = yt_ref.shape[0]
    inv = lax.rsqrt(ss / (nh * HEAD_DIM) + LN_EPS)
    for h in range(nh):
        sl = slice(h * HEAD_DIM, (h + 1) * HEAD_DIM)
        o_ref[:, sl] = ((yt_ref[h] * inv).T * g_ref[:, sl]).astype(BF16)


def _attn_scratch(nh):
    return [pltpu.VMEM((nh, HEAD_DIM, TQ), F32)]


def _attn_in_specs(nh, seq, q_base, k_base, v_base, bias_base, prefetch_kv):
    nb = seq // TQ
    assert q_base % nh == 0 and k_base % nh == 0 and v_base % nh == 0 and bias_base % nh == 0
    kv_spec = pl.BlockSpec if prefetch_kv else _resident
    return [pl.BlockSpec(memory_space=pltpu.SMEM),
            pl.BlockSpec((nh, TQ, HEAD_DIM), lambda b, i: (q_base // nh, b * nb + i, 0)),
            kv_spec((nh, seq, HEAD_DIM), lambda b, i: (k_base // nh, b, 0)),
            kv_spec((nh, nb, VT_ROWS, TQ), lambda b, i: (v_base // nh, b, 0, 0)),
            _resident((nh, 2, TQ, TQ), lambda b, i: (bias_base // nh, 0, 0, 0))]


def _moba_kernel(head0, tab_ref, q_ref, k_ref, vt_ref, bias_ref, g_ref, o_ref,
                 kbar_ref, msel_ref, yt_ref):
    i = pl.program_id(1)
    nh = q_ref.shape[0]
    nb = k_ref.shape[1] // TQ
    hg = msel_ref.shape[0]

    @pl.when(i == 0)
    def _():
        def kbar_body(h, carry):
            for n in range(nb):
                blk = k_ref[h, n * TQ:(n + 1) * TQ, :].astype(F32)
                kbar_ref[h, n:n + 1, :] = jnp.mean(blk, axis=0, keepdims=True)
            return carry
        lax.fori_loop(0, nh, kbar_body, 0)

    blk_idx = lax.broadcasted_iota(I32, (nb, TQ), 0)
    past = blk_idx < i

    def group_body(grp, ss):
        heads = [grp * hg + g for g in range(hg)]
        for g, h in enumerate(heads):
            kb = kbar_ref[h]
            hi = kb.astype(BF16)
            lo = (kb - hi.astype(F32)).astype(BF16)
            both = lax.dot_general(jnp.concatenate([hi, lo], axis=0), q_ref[h], NT_DIMS,
                                   preferred_element_type=F32)
            gate = both[:nb] + both[nb:]
            gate = jnp.where(past, gate, NEG)
            rank = jnp.zeros((nb, TQ), F32)
            for mm in range(nb):
                row = gate[mm:mm + 1, :]
                rank = rank + jnp.where(blk_idx > mm, jnp.where(row >= gate, 1.0, 0.0),
                                        jnp.where(row > gate, 1.0, 0.0))
            keep = jnp.logical_and(past, rank < MOBA_TOPK)
            msel = jnp.where(keep, 0.0, NEG)
            for n in range(nb):
                msel_ref[g, n] = msel[n:n + 1, :]
        ss_g = _attend_heads(i, heads, q_ref, k_ref, vt_ref, bias_ref,
                             lambda h: tab_ref[NUM_BUCKETS - 1, head0 + h] * INV_SCALE, yt_ref,
                             row_mask=lambda g, n: msel_ref[g, n])
        return ss + ss_g

    ss = lax.fori_loop(0, nh // hg, group_body, jnp.zeros((1, TQ), F32))
    _group_rms_store(yt_ref, ss, g_ref, o_ref)


def _moba(rel_table, qk_heads, vt_tiles, bias, g, batch, seq, nh, q_base, k_base, v_base, bias_base):
    nb = seq // TQ
    hg = min(HEADS_INTERLEAVED, nh)
    return pl.pallas_call(
        functools.partial(_moba_kernel, bias_base),
        grid=(batch, nb),
        in_specs=_attn_in_specs(nh, seq, q_base, k_base, v_base, bias_base, prefetch_kv=True)
        + [pl.BlockSpec((1, nh * HEAD_DIM), lambda b, i: (0, 0))],
        out_specs=pl.BlockSpec((TQ, nh * HEAD_DIM), lambda b, i: (b * nb + i, 0)),
        out_shape=jax.ShapeDtypeStruct((batch * seq, nh * HEAD_DIM), BF16),
        scratch_shapes=[pltpu.VMEM((nh, nb, HEAD_DIM), F32),
                        pltpu.VMEM((hg, nb, 1, TQ), F32)] + _attn_scratch(nh),
        compiler_params=_params(2),
        name="moba_attn",
    )(rel_table, qk_heads, qk_heads, vt_tiles, bias, g)


def _sort_key(x):
    bits = pltpu.bitcast(x, I32)
    return jnp.where(bits >= 0, bits, bits ^ jnp.int32(0x7FFFFFFF))


def _dsa_kernel(head0, tab_ref, q_ref, k_ref, vt_ref, bias_ref, qi_ref, ki_ref, wi_ref, g_ref, o_ref,
                key_ref, mask_ref, yt_ref):
    i = pl.program_id(1)
    nh = q_ref.shape[0]
    hg = min(HEADS_INTERLEAVED, nh)
    s_idx = lax.broadcasted_iota(I32, (TQ, TQ), 0)
    t_idx = lax.broadcasted_iota(I32, (TQ, TQ), 1)
    neg_key = _sort_key(jnp.full((TQ, TQ), NEG, F32))

    def score_body(n, carry):
        off = pl.multiple_of(n * TQ, TQ)
        sc = jnp.zeros((TQ, TQ), F32)
        for hh in range(IDX_HEADS):
            kin = ki_ref[hh % 2, pl.ds(off, TQ), :]
            qh = qi_ref[:, (hh // 2) * 2 * IDX_DIM:(hh // 2 + 1) * 2 * IDX_DIM]
            s = lax.dot_general(kin, qh, NT_DIMS, preferred_element_type=F32)
            sc = sc + jnp.maximum(s, 0.0) * wi_ref[hh:hh + 1, :]
        sc = sc + 0.0
        admissible = jnp.logical_or(n < i, s_idx <= t_idx)
        key_ref[n] = jnp.where(admissible, _sort_key(sc), neg_key)
        return carry

    lax.fori_loop(0, i + 1, score_body, 0)

    def count_ge(cand):
        def cnt_body(n, part):
            c = jnp.where(key_ref[n] >= cand, 1.0, 0.0)
            return part + jnp.sum(c.reshape(TQ // SUBLANES, SUBLANES, TQ), axis=0)
        part = lax.fori_loop(0, i + 1, cnt_body, jnp.zeros((SUBLANES, TQ), F32))
        return jnp.sum(part, axis=0, keepdims=True)

    def bisect(it, thr):
        cand = thr + jnp.left_shift(jnp.int32(1), 31 - it)
        return jnp.where(count_ge(cand) >= DSA_TOPK, cand, thr)

    thr = lax.fori_loop(0, 32, bisect, jnp.full((1, TQ), INT_MIN, I32))
    ties_wanted = DSA_TOPK - count_ge(thr + 1)
    lower = jnp.where(t_idx < s_idx, 1.0, 0.0).astype(BF16)

    def mask_body(n, seen):
        kk = key_ref[n]
        eq = kk == thr
        eqf = jnp.where(eq, 1.0, 0.0)
        before = jnp.dot(lower, eqf.astype(BF16), preferred_element_type=F32) + seen
        keep_tie = jnp.where(before < ties_wanted, 0.0, NEG)
        mask_ref[n] = jnp.where(kk > thr, 0.0, jnp.where(eq, keep_tie, NEG))
        return seen + jnp.sum(eqf, axis=0, keepdims=True)

    lax.fori_loop(0, i + 1, mask_body, jnp.zeros((1, TQ), F32))

    def group_body(grp, ss):
        heads = [grp * hg + g for g in range(hg)]
        ss_g = _attend_heads(i, heads, q_ref, k_ref, vt_ref, bias_ref,
                             lambda h: tab_ref[NUM_BUCKETS - 1, head0 + h] * INV_SCALE, yt_ref,
                             elem_mask=lambda n: mask_ref[n])
        return ss + ss_g

    ss = lax.fori_loop(0, nh // hg, group_body, jnp.zeros((1, TQ), F32))
    _group_rms_store(yt_ref, ss, g_ref, o_ref)


def _dsa(rel_table, qk_heads, vt_tiles, bias, qi, ki, wi, g, batch, seq, nh,
         q_base, k_base, v_base, bias_base):
    nb = seq // TQ
    return pl.pallas_call(
        functools.partial(_dsa_kernel, bias_base),
        grid=(batch, nb),
        in_specs=_attn_in_specs(nh, seq, q_base, k_base, v_base, bias_base, prefetch_kv=False)
        + [pl.BlockSpec((TQ, IDX_HEADS * IDX_DIM), lambda b, i: (b * nb + i, 0)),
           _resident((2, seq, 2 * IDX_DIM), lambda b, i: (0, b, 0)),
           pl.BlockSpec((IDX_HEADS, TQ), lambda b, i: (0, b * nb + i)),
           pl.BlockSpec((1, nh * HEAD_DIM), lambda b, i: (0, 0))],
        out_specs=pl.BlockSpec((TQ, nh * HEAD_DIM), lambda b, i: (b * nb + i, 0)),
        out_shape=jax.ShapeDtypeStruct((batch * seq, nh * HEAD_DIM), BF16),
        scratch_shapes=[pltpu.VMEM((nb, TQ, TQ), I32),
                        pltpu.VMEM((nb, TQ, TQ), F32)] + _attn_scratch(nh),
        compiler_params=_params(2),
        name="dsa_attn",
    )(rel_table, qk_heads, qk_heads, vt_tiles, bias, qi, ki, wi, g)


def _out_proj_kernel(a_ref, b_ref, wa_ref, wb_ref, x_ref, o_ref, wbf_ref):
    ka = a_ref.shape[1]

    @pl.when(pl.program_id(1) == 0)
    def _():
        wbf_ref[:ka, :] = wa_ref[...].astype(BF16)
        wbf_ref[ka:, :] = wb_ref[...].astype(BF16)

    acc = jnp.dot(a_ref[...], wbf_ref[:ka, :], preferred_element_type=F32)
    acc = acc + jnp.dot(b_ref[...], wbf_ref[ka:, :], preferred_element_type=F32)
    o_ref[...] = ALPHA * x_ref[...] + acc


def _out_proj(ya, yb, w_out, x2d):
    m, ka = ya.shape
    kb = yb.shape[1]
    assert ka == kb
    n = w_out.shape[1]
    tm = _pick(m, (1024, 512, 256))
    tn = _pick(n, (512, 256, 128))
    return pl.pallas_call(
        _out_proj_kernel,
        grid=(n // tn, m // tm),
        in_specs=[pl.BlockSpec((tm, ka), lambda j, i: (i, 0)),
                  pl.BlockSpec((tm, kb), lambda j, i: (i, 0)),
                  pl.BlockSpec((ka, tn), lambda j, i: (0, j)),
                  pl.BlockSpec((kb, tn), lambda j, i: (1, j)),
                  pl.BlockSpec((tm, tn), lambda j, i: (i, j))],
        out_specs=pl.BlockSpec((tm, tn), lambda j, i: (i, j)),
        out_shape=jax.ShapeDtypeStruct((m, n), F32),
        scratch_shapes=[pltpu.VMEM((ka + kb, tn), BF16)],
        compiler_params=_params(2),
        name="out_proj",
    )(ya, yb, w_out, w_out, x2d)


def _ln_kernel(z_ref, g_ref, b_ref, *o_refs):
    z = z_ref[...]
    mu = jnp.mean(z, axis=1, keepdims=True)
    zc = z - mu
    var = jnp.mean(zc * zc, axis=1, keepdims=True)
    y = zc * lax.rsqrt(var + LN_EPS) * g_ref[...] + b_ref[...]
    for o_ref in o_refs:
        o_ref[...] = y.astype(o_ref.dtype)


def _layer_norm(z, g, b, out_dtypes):
    m, d = z.shape
    tm = _pick(m, (512, 256, 128))
    row = pl.BlockSpec((tm, d), lambda i: (i, 0))
    vec = pl.BlockSpec((1, d), lambda i: (0, 0))
    return pl.pallas_call(
        _ln_kernel,
        grid=(m // tm,),
        in_specs=[row, vec, vec],
        out_specs=[row for _ in out_dtypes],
        out_shape=[jax.ShapeDtypeStruct((m, d), dt) for dt in out_dtypes],
        compiler_params=_params(1),
        name="layer_norm",
    )(z, g.reshape(1, d), b.reshape(1, d))


def _causal_conv(u, history, cw_ref, cb_ref):
    ext = jnp.concatenate([history, u], axis=0)
    out = None
    for j in range(CONV_WIDTH):
        back = CONV_WIDTH - 1 - j
        uj = u if back == 0 else pltpu.roll(ext, back, axis=0)[SUBLANES:]
        term = cw_ref[j:j + 1, :] * uj
        out = term if out is None else out + term
    return out + cb_ref[...]


def _ffn_up_kernel(x_ref, wv_ref, wg_ref, cwv_ref, cwg_ref, cbv_ref, cbg_ref, o_ref):
    wv = wv_ref[...].astype(BF16)
    wg = wg_ref[...].astype(BF16)
    n_chunks = x_ref.shape[0] // FFN_ROW_CHUNK

    def matmuls(c):
        x = x_ref[c * FFN_ROW_CHUNK:(c + 1) * FFN_ROW_CHUNK, :]
        return (jnp.dot(x, wv, preferred_element_type=F32), jnp.dot(x, wg, preferred_element_type=F32))

    pending = matmuls(0)
    history = (jnp.zeros((SUBLANES, wv.shape[1]), F32),) * 2
    for c in range(n_chunks):
        uv, ug = pending
        if c + 1 < n_chunks:
            pending = matmuls(c + 1)
        val = _causal_conv(uv, history[0], cwv_ref, cbv_ref)
        gate = _causal_conv(ug, history[1], cwg_ref, cbg_ref)
        o_ref[c * FFN_ROW_CHUNK:(c + 1) * FFN_ROW_CHUNK, :] = (
            val * (gate * (1.0 / (1.0 + jnp.exp(-gate))))).astype(BF16)
        history = (uv[-SUBLANES:], ug[-SUBLANES:])


def _ffn_up(x1_bf, w_up, conv_w, conv_b, batch, seq):
    m, d = x1_bf.shape
    dff = w_up.shape[1] // 2
    tn = _pick(dff, (256, 128))
    nj = dff // tn
    cb = conv_b.reshape(1, 2 * dff)
    return pl.pallas_call(
        _ffn_up_kernel,
        grid=(batch, nj),
        in_specs=[_resident((seq, d), lambda b, j: (b, 0)),
                  pl.BlockSpec((d, tn), lambda b, j: (0, j)),
                  pl.BlockSpec((d, tn), lambda b, j: (0, nj + j)),
                  pl.BlockSpec((CONV_WIDTH, tn), lambda b, j: (0, j)),
                  pl.BlockSpec((CONV_WIDTH, tn), lambda b, j: (0, nj + j)),
                  pl.BlockSpec((1, tn), lambda b, j: (0, j)),
                  pl.BlockSpec((1, tn), lambda b, j: (0, nj + j))],
        out_specs=pl.BlockSpec((seq, tn), lambda b, j: (b, j)),
        out_shape=jax.ShapeDtypeStruct((m, dff), BF16),
        compiler_params=_params(2),
        name="ffn_up",
    )(x1_bf, w_up, w_up, conv_w, conv_w, cb, cb)


def _ffn_down_kernel(h_ref, w_ref, x_ref, o_ref):
    o_ref[...] = ALPHA * x_ref[...] + jnp.dot(h_ref[...], w_ref[...], preferred_element_type=F32)


def _ffn_down(h, w_down_bf, x1):
    m, k = h.shape
    n = w_down_bf.shape[1]
    tm = _pick(m, (512, 256))
    tn = _pick(n, (512, 256, 128))
    return pl.pallas_call(
        _ffn_down_kernel,
        grid=(n // tn, m // tm),
        in_specs=[pl.BlockSpec((tm, k), lambda j, i: (i, 0)),
                  pl.BlockSpec((k, tn), lambda j, i: (0, j)),
                  pl.BlockSpec((tm, tn), lambda j, i: (i, j))],
        out_specs=pl.BlockSpec((tm, tn), lambda j, i: (i, j)),
        out_shape=jax.ShapeDtypeStruct((m, n), F32),
        compiler_params=_params(2),
        name="ffn_down",
    )(h, w_down_bf, x1)


def kernel(x, w_in, rel_table, g_moba, g_dsa, w_out, ln1_g, ln1_b, w_up, conv_w, conv_b, w_down,
           ln2_g, ln2_b):
    batch, seq, d = x.shape
    m = batch * seq
    nh = d // HEAD_DIM // 2
    wg = nh * HEAD_DIM
    n_main = 6 * wg
    assert w_in.shape == (d, n_main + N_IDX) and seq % TQ == 0 and seq // 4 >= DSA_TOPK

    x2d = x.reshape(m, d)
    x_bf = x2d.astype(BF16)
    w_t = w_in.T

    qk_heads = _proj_qk(x_bf, w_t, wg)
    vt_tiles = _proj_vt(x_bf, w_t, wg)
    qi, ki, wi = _proj_idx(x_bf, w_t, n_main)
    bias = _bias_tiles(rel_table)

    ya = _moba(rel_table, qk_heads, vt_tiles, bias, g_moba.reshape(1, wg), batch, seq, nh,
               0, nh, 0, 0)
    yb = _dsa(rel_table, qk_heads, vt_tiles, bias, qi, ki, wi, g_dsa.reshape(1, wg), batch, seq, nh,
              2 * nh, 3 * nh, nh, nh)

    z1 = _out_proj(ya, yb, w_out, x2d)
    x1, x1_bf = _layer_norm(z1, ln1_g, ln1_b, (F32, BF16))

    hmid = _ffn_up(x1_bf, w_up, conv_w, conv_b, batch, seq)
    z2 = _ffn_down(hmid, w_down.astype(BF16), x1)
    (out,) = _layer_norm(z2, ln2_g, ln2_b, (F32,))
    return out.reshape(batch, seq, d)
```

```python
import functools
import math

import jax
import jax.numpy as jnp
from jax import lax
from jax.experimental import pallas as pl
from jax.experimental.pallas import tpu as pltpu

F32 = jnp.float32
BF16 = jnp.bfloat16
I32 = jnp.int32

HEAD_DIM = 128
MOBA_BLOCK = 256
MOBA_TOPK = 3
IDX_HEADS = 16
IDX_DIM = 64
DSA_TOPK = 256
NUM_BUCKETS = 32
MAX_DISTANCE = 128
CONV_WIDTH = 3
DEPTH = 1
ALPHA = (2.0 * DEPTH) ** 0.25
NEG = -1e30
LN_EPS = 1e-5

TQ = MOBA_BLOCK
LANES = 128
SUBLANES = 8
VMEM_LIMIT = 56 * 1024 * 1024
DSA_VMEM_LIMIT = 61 * 1024 * 1024
HEADS_INTERLEAVED = 16
SCORE_LOOKAHEAD = 4
BF16_SUBLANES = 16
VT_ROWS = HEAD_DIM + BF16_SUBLANES
FFN_ROW_CHUNK = 512

SCALE = HEAD_DIM ** -0.5
INV_SCALE = HEAD_DIM ** 0.5
EXP2_COEF = SCALE * math.log2(math.e)
NT_DIMS = (((1,), (1,)), ((), ()))
INT_MIN = -(2 ** 31)
N_IDX = IDX_HEADS * IDX_DIM + IDX_DIM + IDX_HEADS


def _params(n_axes, vmem=VMEM_LIMIT):
    return pltpu.CompilerParams(dimension_semantics=("arbitrary",) * n_axes,
                                vmem_limit_bytes=vmem)


def _pick(n, candidates):
    for c in candidates:
        if n % c == 0:
            return c
    raise ValueError(f"no tile for {n} among {candidates}")


def _resident(block_shape, index_map):
    return pl.BlockSpec(block_shape, index_map, pipeline_mode=pl.Buffered(1))


def _proj_rows_kernel(x_ref, w_ref, o_ref, wbf_ref):
    @pl.when(pl.program_id(1) == 0)
    def _():
        wbf_ref[...] = w_ref[...].astype(BF16)

    acc = lax.dot_general(x_ref[...], wbf_ref[...], NT_DIMS, preferred_element_type=F32)
    for c in range(o_ref.shape[0]):
        o_ref[c] = acc[:, c * HEAD_DIM:(c + 1) * HEAD_DIM].astype(BF16)


def _proj_cols_kernel(x_ref, w_ref, o_ref, wbf_ref):
    @pl.when(pl.program_id(1) == 0)
    def _():
        wbf_ref[...] = w_ref[...].astype(BF16)

    acc = lax.dot_general(wbf_ref[...], x_ref[...], NT_DIMS, preferred_element_type=F32)
    ones = jnp.ones((VT_ROWS - HEAD_DIM, TQ), BF16)
    for c in range(o_ref.shape[0]):
        for t in range(o_ref.shape[1]):
            o_ref[c, t, :HEAD_DIM, :] = acc[c * HEAD_DIM:(c + 1) * HEAD_DIM, t * TQ:(t + 1) * TQ].astype(BF16)
            o_ref[c, t, HEAD_DIM:, :] = ones


def _proj_qk(x_bf, w_t, wg):
    m, d = x_bf.shape
    tm = _pick(m, (1024, 512, 256))
    tn = _pick(wg, (512, 256, 128))
    tpp = wg // tn
    hp = tn // HEAD_DIM

    def w_map(j, i):
        part = j // tpp
        return ((part + (part >= 2)) * tpp + j % tpp, 0)

    return pl.pallas_call(
        _proj_rows_kernel,
        grid=(4 * tpp, m // tm),
        in_specs=[pl.BlockSpec((tm, d), lambda j, i: (i, 0)),
                  pl.BlockSpec((tn, d), w_map)],
        out_specs=pl.BlockSpec((hp, tm, HEAD_DIM), lambda j, i: (j, i, 0)),
        out_shape=jax.ShapeDtypeStruct((4 * wg // HEAD_DIM, m, HEAD_DIM), BF16),
        scratch_shapes=[pltpu.VMEM((tn, d), BF16)],
        compiler_params=_params(2),
        name="proj_qk",
    )(x_bf, w_t)


def _proj_vt(x_bf, w_t, wg):
    m, d = x_bf.shape
    tm = _pick(m, (1024, 512, 256))
    tn = _pick(wg, (512, 256, 128))
    tpp = wg // tn
    hp = tn // HEAD_DIM
    return pl.pallas_call(
        _proj_cols_kernel,
        grid=(2 * tpp, m // tm),
        in_specs=[pl.BlockSpec((tm, d), lambda j, i: (i, 0)),
                  pl.BlockSpec((tn, d), lambda j, i: ((2 + 3 * (j // tpp)) * tpp + j % tpp, 0))],
        out_specs=pl.BlockSpec((hp, tm // TQ, VT_ROWS, TQ), lambda j, i: (j, i, 0, 0)),
        out_shape=jax.ShapeDtypeStruct((2 * wg // HEAD_DIM, m // TQ, VT_ROWS, TQ), BF16),
        scratch_shapes=[pltpu.VMEM((tn, d), BF16)],
        compiler_params=_params(2),
        name="proj_vt",
    )(x_bf, w_t)


def _proj_idx_kernel(x_ref, wq_ref, wk_ref, ww_ref, qi_ref, ki_ref, wi_ref):
    x = x_ref[...]
    qi = lax.dot_general(x, wq_ref[...].astype(BF16), NT_DIMS, preferred_element_type=F32)
    qi_ref[...] = (qi * IDX_DIM ** -0.5).astype(BF16)
    wk = wk_ref[...].astype(BF16)
    ww = ww_ref[...].astype(BF16)
    pad = jnp.zeros((LANES - IDX_DIM - IDX_HEADS, wk.shape[1]), BF16)
    kw = lax.dot_general(x, jnp.concatenate([wk, ww, pad], axis=0), NT_DIMS, preferred_element_type=F32)
    lane = lax.broadcasted_iota(I32, kw.shape, 1)
    k_lo = jnp.where(lane < IDX_DIM, kw, 0.0)
    ki_ref[0] = k_lo.astype(BF16)
    ki_ref[1] = pltpu.roll(k_lo, IDX_DIM, axis=1).astype(BF16)
    wi_ref[...] = kw.T[IDX_DIM:IDX_DIM + IDX_HEADS] * IDX_HEADS ** -0.5


def _proj_idx(x_bf, w_t, n_main):
    m, d = x_bf.shape
    tm = _pick(m, (1024, 512, 256))
    nq = IDX_HEADS * IDX_DIM
    assert n_main % nq == 0 and (n_main + nq) % IDX_DIM == 0 and (n_main + nq + IDX_DIM) % IDX_HEADS == 0
    return pl.pallas_call(
        _proj_idx_kernel,
        grid=(m // tm,),
        in_specs=[pl.BlockSpec((tm, d), lambda i: (i, 0)),
                  _resident((nq, d), lambda i: (n_main // nq, 0)),
                  _resident((IDX_DIM, d), lambda i: ((n_main + nq) // IDX_DIM, 0)),
                  _resident((IDX_HEADS, d), lambda i: ((n_main + nq + IDX_DIM) // IDX_HEADS, 0))],
        out_specs=[pl.BlockSpec((tm, nq), lambda i: (i, 0)),
                   pl.BlockSpec((2, tm, 2 * IDX_DIM), lambda i: (0, i, 0)),
                   pl.BlockSpec((IDX_HEADS, tm), lambda i: (0, i))],
        out_shape=[jax.ShapeDtypeStruct((m, nq), BF16),
                   jax.ShapeDtypeStruct((2, m, 2 * IDX_DIM), BF16),
                   jax.ShapeDtypeStruct((IDX_HEADS, m), F32)],
        compiler_params=_params(1),
        name="proj_idx",
    )(x_bf, w_t, w_t, w_t)


def _bias_kernel(tab_ref, o_ref):
    h = pl.program_id(0)
    s = lax.broadcasted_iota(I32, (TQ, TQ), 0)
    t = lax.broadcasted_iota(I32, (TQ, TQ), 1)
    max_exact = NUM_BUCKETS // 2
    for slot in range(2):
        dist = (1 - slot) * TQ + t - s
        d = jnp.maximum(dist, 0)
        df = jnp.maximum(d, max_exact).astype(F32)
        large = max_exact + (jnp.log(df / max_exact) / math.log(MAX_DISTANCE / max_exact)
                             * (NUM_BUCKETS - max_exact)).astype(I32)
        large = jnp.minimum(large, NUM_BUCKETS - 1)
        bucket = jnp.where(d < max_exact, d, large)
        out = jnp.zeros((TQ, TQ), F32)
        for k in range(NUM_BUCKETS):
            out = jnp.where(bucket == k, tab_ref[k, h], out)
        o_ref[0, slot] = jnp.where(dist >= 0, out * INV_SCALE, NEG)


def _bias_tiles(rel_table):
    nh = rel_table.shape[1]
    return pl.pallas_call(
        _bias_kernel,
        grid=(nh,),
        in_specs=[pl.BlockSpec(memory_space=pltpu.SMEM)],
        out_specs=pl.BlockSpec((1, 2, TQ, TQ), lambda h: (h, 0, 0, 0)),
        out_shape=jax.ShapeDtypeStruct((nh, 2, TQ, TQ), F32),
        compiler_params=_params(1),
        name="bias_tiles",
    )(rel_table)


def _softmax_step(state, u, row_term, vt):
    tile_max = jnp.max(u, axis=0, keepdims=True)
    if row_term is not None:
        tile_max = tile_max + row_term
    m_new = tile_max if state is None else jnp.maximum(state[0], tile_max)
    off = m_new if row_term is None else m_new - row_term
    p = jnp.exp2((u - off) * EXP2_COEF)
    pv = jnp.dot(vt, p.astype(BF16), preferred_element_type=F32)
    if state is None:
        return m_new, pv
    m_old, acc_old = state
    alpha = jnp.exp2((m_old - m_new) * EXP2_COEF)
    return m_new, alpha * acc_old + pv


def _pipelined(n_slots, issue, consume):
    pending = {g: issue(g) for g in range(min(SCORE_LOOKAHEAD, n_slots))}
    out = []
    for g in range(n_slots):
        if g + SCORE_LOOKAHEAD < n_slots:
            pending[g + SCORE_LOOKAHEAD] = issue(g + SCORE_LOOKAHEAD)
        out.append(consume(g, pending.pop(g)))
    return tuple(out)


def _attend_heads(i, heads, q_ref, k_ref, vt_ref, bias_ref, far_bias, yt_ref,
                  elem_mask=None, row_mask=None):
    def logits(h, n):
        kt = k_ref[h, pl.ds(pl.multiple_of(n * TQ, TQ), TQ), :]
        u = lax.dot_general(kt, q_ref[h], NT_DIMS, preferred_element_type=F32)
        return u if elem_mask is None else u + elem_mask(n)

    ns = len(heads)

    states = _pipelined(
        ns, lambda g: logits(heads[g], i) + bias_ref[heads[g], 1],
        lambda g, u: _softmax_step(None, u, None, vt_ref[heads[g], i]))

    def adjacent(states):
        def consume(g, u):
            rt = None if row_mask is None else row_mask(g, i - 1)
            return _softmax_step(states[g], u, rt, vt_ref[heads[g], i - 1])
        return _pipelined(ns, lambda g: logits(heads[g], i - 1) + bias_ref[heads[g], 0], consume)

    states = lax.cond(i >= 1, adjacent, lambda s: s, states)

    def far_body(n, states):
        def consume(g, u):
            fb = far_bias(heads[g])
            rt = fb if row_mask is None else fb + row_mask(g, n)
            return _softmax_step(states[g], u, rt, vt_ref[heads[g], n])
        return _pipelined(ns, lambda g: logits(heads[g], n), consume)

    states = lax.fori_loop(0, i - 1, far_body, states)

    ss = jnp.zeros((1, TQ), F32)
    for (m, acc), h in zip(states, heads):
        yt = acc[:HEAD_DIM] / acc[HEAD_DIM:HEAD_DIM + 1]
        yt_ref[h] = yt
        ss = ss + jnp.sum(yt * yt, axis=0, keepdims=True)
    return ss


def _group_rms_store(yt_ref, ss, g_ref, o_ref):
    nh = yt_ref.shape[0]
    inv = lax.rsqrt(ss / (nh * HEAD_DIM) + LN_EPS)
    for h in range(nh):
        sl = slice(h * HEAD_DIM, (h + 1) * HEAD_DIM)
        o_ref[:, sl] = ((yt_ref[h] * inv).T * g_ref[:, sl]).astype(BF16)


def _attn_scratch(nh):
    return [pltpu.VMEM((nh, HEAD_DIM, TQ), F32)]


def _attn_in_specs(nh, seq, q_base, k_base, v_base, bias_base, prefetch_kv):
    nb = seq // TQ
    assert q_base % nh == 0 and k_base % nh == 0 and v_base % nh == 0 and bias_base % nh == 0
    kv_spec = pl.BlockSpec if prefetch_kv else _resident
    return [pl.BlockSpec(memory_space=pltpu.SMEM),
            pl.BlockSpec((nh, TQ, HEAD_DIM), lambda b, i: (q_base // nh, b * nb + i, 0)),
            kv_spec((nh, seq, HEAD_DIM), lambda b, i: (k_base // nh, b, 0)),
            kv_spec((nh, nb, VT_ROWS, TQ), lambda b, i: (v_base // nh, b, 0, 0)),
            _resident((nh, 2, TQ, TQ), lambda b, i: (bias_base // nh, 0, 0, 0))]


def _moba_kernel(head0, tab_ref, q_ref, k_ref, vt_ref, bias_ref, g_ref, o_ref,
                 kbar_ref, msel_ref, yt_ref):
    i = pl.program_id(1)
    nh = q_ref.shape[0]
    nb = k_ref.shape[1] // TQ
    hg = msel_ref.shape[0]

    @pl.when(i == 0)
    def _():
        def kbar_body(h, carry):
            for n in range(nb):
                blk = k_ref[h, n * TQ:(n + 1) * TQ, :].astype(F32)
                kbar_ref[h, n:n + 1, :] = jnp.mean(blk, axis=0, keepdims=True)
            return carry
        lax.fori_loop(0, nh, kbar_body, 0)

    blk_idx = lax.broadcasted_iota(I32, (nb, TQ), 0)
    past = blk_idx < i

    def group_body(grp, ss):
        heads = [grp * hg + g for g in range(hg)]
        for g, h in enumerate(heads):
            kb = kbar_ref[h]
            hi = kb.astype(BF16)
            lo = (kb - hi.astype(F32)).astype(BF16)
            both = lax.dot_general(jnp.concatenate([hi, lo], axis=0), q_ref[h], NT_DIMS,
                                   preferred_element_type=F32)
            gate = both[:nb] + both[nb:]
            gate = jnp.where(past, gate, NEG)
            rank = jnp.zeros((nb, TQ), F32)
            for mm in range(nb):
                row = gate[mm:mm + 1, :]
                rank = rank + jnp.where(blk_idx > mm, jnp.where(row >= gate, 1.0, 0.0),
                                        jnp.where(row > gate, 1.0, 0.0))
            keep = jnp.logical_and(past, rank < MOBA_TOPK)
            msel = jnp.where(keep, 0.0, NEG)
            for n in range(nb):
                msel_ref[g, n] = msel[n:n + 1, :]
        ss_g = _attend_heads(i, heads, q_ref, k_ref, vt_ref, bias_ref,
                             lambda h: tab_ref[NUM_BUCKETS - 1, head0 + h] * INV_SCALE, yt_ref,
                             row_mask=lambda g, n: msel_ref[g, n])
        return ss + ss_g

    ss = lax.fori_loop(0, nh // hg, group_body, jnp.zeros((1, TQ), F32))
    _group_rms_store(yt_ref, ss, g_ref, o_ref)


def _moba(rel_table, qk_heads, vt_tiles, bias, g, batch, seq, nh, q_base, k_base, v_base, bias_base):
    nb = seq // TQ
    hg = min(HEADS_INTERLEAVED, nh)
    return pl.pallas_call(
        functools.partial(_moba_kernel, bias_base),
        grid=(batch, nb),
        in_specs=_attn_in_specs(nh, seq, q_base, k_base, v_base, bias_base, prefetch_kv=True)
        + [pl.BlockSpec((1, nh * HEAD_DIM), lambda b, i: (0, 0))],
        out_specs=pl.BlockSpec((TQ, nh * HEAD_DIM), lambda b, i: (b * nb + i, 0)),
        out_shape=jax.ShapeDtypeStruct((batch * seq, nh * HEAD_DIM), BF16),
        scratch_shapes=[pltpu.VMEM((nh, nb, HEAD_DIM), F32),
                        pltpu.VMEM((hg, nb, 1, TQ), F32)] + _attn_scratch(nh),
        compiler_params=_params(2),
        name="moba_attn",
    )(rel_table, qk_heads, qk_heads, vt_tiles, bias, g)


def _sort_key(x):
    bits = pltpu.bitcast(x, I32)
    return jnp.where(bits >= 0, bits, bits ^ jnp.int32(0x7FFFFFFF))


def _dsa_kernel(head0, tab_ref, q_ref, k_ref, vt_ref, bias_ref, qi_ref, ki_ref, wi_ref, g_ref, o_ref,
                key_ref, mask_ref, yt_ref):
    i = pl.program_id(1)
    nh = q_ref.shape[0]
    hg = min(HEADS_INTERLEAVED, nh)
    s_idx = lax.broadcasted_iota(I32, (TQ, TQ), 0)
    t_idx = lax.broadcasted_iota(I32, (TQ, TQ), 1)
    neg_key = _sort_key(jnp.full((TQ, TQ), NEG, F32))

    def score_body(n, carry):
        off = pl.multiple_of(n * TQ, TQ)
        sc = jnp.zeros((TQ, TQ), F32)
        for hh in range(IDX_HEADS):
            kin = ki_ref[hh % 2, pl.ds(off, TQ), :]
            qh = qi_ref[:, (hh // 2) * 2 * IDX_DIM:(hh // 2 + 1) * 2 * IDX_DIM]
            s = lax.dot_general(kin, qh, NT_DIMS, preferred_element_type=F32)
            sc = sc + jnp.maximum(s, 0.0) * wi_ref[hh:hh + 1, :]
        sc = sc + 0.0
        admissible = jnp.logical_or(n < i, s_idx <= t_idx)
        key_ref[n] = jnp.where(admissible, _sort_key(sc), neg_key)
        return carry

    lax.fori_loop(0, i + 1, score_body, 0)

    def count_ge(cand):
        def cnt_body(n, part):
            c = jnp.where(key_ref[n] >= cand, 1.0, 0.0)
            return part + jnp.sum(c.reshape(TQ // SUBLANES, SUBLANES, TQ), axis=0)
        part = lax.fori_loop(0, i + 1, cnt_body, jnp.zeros((SUBLANES, TQ), F32))
        return jnp.sum(part, axis=0, keepdims=True)

    def bisect(it, thr):
        cand = thr + jnp.left_shift(jnp.int32(1), 31 - it)
        return jnp.where(count_ge(cand) >= DSA_TOPK, cand, thr)

    thr = lax.fori_loop(0, 32, bisect, jnp.full((1, TQ), INT_MIN, I32))
    ties_wanted = DSA_TOPK - count_ge(thr + 1)
    lower = jnp.where(t_idx < s_idx, 1.0, 0.0).astype(BF16)

    def mask_body(n, seen):
        kk = key_ref[n]
        eq = kk == thr
        eqf = jnp.where(eq, 1.0, 0.0)
        before = jnp.dot(lower, eqf.astype(BF16), preferred_element_type=F32) + seen
        keep_tie = jnp.where(before < ties_wanted, 0.0, NEG)
        mask_ref[n] = jnp.where(kk > thr, 0.0, jnp.where(eq, keep_tie, NEG))
        return seen + jnp.sum(eqf, axis=0, keepdims=True)

    lax.fori_loop(0, i + 1, mask_body, jnp.zeros((1, TQ), F32))

    def group_body(grp, ss):
        heads = [grp * hg + g for g in range(hg)]
        ss_g = _attend_heads(i, heads, q_ref, k_ref, vt_ref, bias_ref,
                             lambda h: tab_ref[NUM_BUCKETS - 1, head0 + h] * INV_SCALE, yt_ref,
                             elem_mask=lambda n: mask_ref[n])
        return ss + ss_g

    ss = lax.fori_loop(0, nh // hg, group_body, jnp.zeros((1, TQ), F32))
    _group_rms_store(yt_ref, ss, g_ref, o_ref)


def _dsa(rel_table, qk_heads, vt_tiles, bias, qi, ki, wi, g, batch, seq, nh,
         q_base, k_base, v_base, bias_base):
    nb = seq // TQ
    return pl.pallas_call(
        functools.partial(_dsa_kernel, bias_base),
        grid=(batch, nb),
        in_specs=_attn_in_specs(nh, seq, q_base, k_base, v_base, bias_base, prefetch_kv=True)
        + [pl.BlockSpec((TQ, IDX_HEADS * IDX_DIM), lambda b, i: (b * nb + i, 0)),
           _resident((2, seq, 2 * IDX_DIM), lambda b, i: (0, b, 0)),
           pl.BlockSpec((IDX_HEADS, TQ), lambda b, i: (0, b * nb + i)),
           pl.BlockSpec((1, nh * HEAD_DIM), lambda b, i: (0, 0))],
        out_specs=pl.BlockSpec((TQ, nh * HEAD_DIM), lambda b, i: (b * nb + i, 0)),
        out_shape=jax.ShapeDtypeStruct((batch * seq, nh * HEAD_DIM), BF16),
        scratch_shapes=[pltpu.VMEM((nb, TQ, TQ), I32),
                        pltpu.VMEM((nb, TQ, TQ), F32)] + _attn_scratch(nh),
        compiler_params=_params(2, vmem=DSA_VMEM_LIMIT),
        name="dsa_attn",
    )(rel_table, qk_heads, qk_heads, vt_tiles, bias, qi, ki, wi, g)


def _out_proj_kernel(a_ref, b_ref, wa_ref, wb_ref, x_ref, o_ref, wbf_ref):
    ka = a_ref.shape[1]

    @pl.when(pl.program_id(1) == 0)
    def _():
        wbf_ref[:ka, :] = wa_ref[...].astype(BF16)
        wbf_ref[ka:, :] = wb_ref[...].astype(BF16)

    acc = jnp.dot(a_ref[...], wbf_ref[:ka, :], preferred_element_type=F32)
    acc = acc + jnp.dot(b_ref[...], wbf_ref[ka:, :], preferred_element_type=F32)
    o_ref[...] = ALPHA * x_ref[...] + acc


def _out_proj(ya, yb, w_out, x2d):
    m, ka = ya.shape
    kb = yb.shape[1]
    assert ka == kb
    n = w_out.shape[1]
    tm = _pick(m, (1024, 512, 256))
    tn = _pick(n, (512, 256, 128))
    return pl.pallas_call(
        _out_proj_kernel,
        grid=(n // tn, m // tm),
        in_specs=[pl.BlockSpec((tm, ka), lambda j, i: (i, 0)),
                  pl.BlockSpec((tm, kb), lambda j, i: (i, 0)),
                  pl.BlockSpec((ka, tn), lambda j, i: (0, j)),
                  pl.BlockSpec((kb, tn), lambda j, i: (1, j)),
                  pl.BlockSpec((tm, tn), lambda j, i: (i, j))],
        out_specs=pl.BlockSpec((tm, tn), lambda j, i: (i, j)),
        out_shape=jax.ShapeDtypeStruct((m, n), F32),
        scratch_shapes=[pltpu.VMEM((ka + kb, tn), BF16)],
        compiler_params=_params(2),
        name="out_proj",
    )(ya, yb, w_out, w_out, x2d)


def _ln_kernel(z_ref, g_ref, b_ref, *o_refs):
    z = z_ref[...]
    mu = jnp.mean(z, axis=1, keepdims=True)
    zc = z - mu
    var = jnp.mean(zc * zc, axis=1, keepdims=True)
    y = zc * lax.rsqrt(var + LN_EPS) * g_ref[...] + b_ref[...]
    for o_ref in o_refs:
        o_ref[...] = y.astype(o_ref.dtype)


def _layer_norm(z, g, b, out_dtypes):
    m, d = z.shape
    tm = _pick(m, (512, 256, 128))
    row = pl.BlockSpec((tm, d), lambda i: (i, 0))
    vec = pl.BlockSpec((1, d), lambda i: (0, 0))
    return pl.pallas_call(
        _ln_kernel,
        grid=(m // tm,),
        in_specs=[row, vec, vec],
        out_specs=[row for _ in out_dtypes],
        out_shape=[jax.ShapeDtypeStruct((m, d), dt) for dt in out_dtypes],
        compiler_params=_params(1),
        name="layer_norm",
    )(z, g.reshape(1, d), b.reshape(1, d))


def _causal_conv(u, history, cw_ref, cb_ref):
    ext = jnp.concatenate([history, u], axis=0)
    out = None
    for j in range(CONV_WIDTH):
        back = CONV_WIDTH - 1 - j
        uj = u if back == 0 else pltpu.roll(ext, back, axis=0)[SUBLANES:]
        term = cw_ref[j:j + 1, :] * uj
        out = term if out is None else out + term
    return out + cb_ref[...]


def _ffn_up_kernel(x_ref, wv_ref, wg_ref, cwv_ref, cwg_ref, cbv_ref, cbg_ref, o_ref):
    wv = wv_ref[...].astype(BF16)
    wg = wg_ref[...].astype(BF16)
    n_chunks = x_ref.shape[0] // FFN_ROW_CHUNK

    def matmuls(c):
        x = x_ref[c * FFN_ROW_CHUNK:(c + 1) * FFN_ROW_CHUNK, :]
        return (jnp.dot(x, wv, preferred_element_type=F32), jnp.dot(x, wg, preferred_element_type=F32))

    pending = matmuls(0)
    history = (jnp.zeros((SUBLANES, wv.shape[1]), F32),) * 2
    for c in range(n_chunks):
        uv, ug = pending
        if c + 1 < n_chunks:
            pending = matmuls(c + 1)
        val = _causal_conv(uv, history[0], cwv_ref, cbv_ref)
        gate = _causal_conv(ug, history[1], cwg_ref, cbg_ref)
        o_ref[c * FFN_ROW_CHUNK:(c + 1) * FFN_ROW_CHUNK, :] = (
            val * (gate * (1.0 / (1.0 + jnp.exp(-gate))))).astype(BF16)
        history = (uv[-SUBLANES:], ug[-SUBLANES:])


def _ffn_up(x1_bf, w_up, conv_w, conv_b, batch, seq):
    m, d = x1_bf.shape
    dff = w_up.shape[1] // 2
    tn = _pick(dff, (256, 128))
    nj = dff // tn
    cb = conv_b.reshape(1, 2 * dff)
    return pl.pallas_call(
        _ffn_up_kernel,
        grid=(batch, nj),
        in_specs=[_resident((seq, d), lambda b, j: (b, 0)),
                  pl.BlockSpec((d, tn), lambda b, j: (0, j)),
                  pl.BlockSpec((d, tn), lambda b, j: (0, nj + j)),
                  pl.BlockSpec((CONV_WIDTH, tn), lambda b, j: (0, j)),
                  pl.BlockSpec((CONV_WIDTH, tn), lambda b, j: (0, nj + j)),
                  pl.BlockSpec((1, tn), lambda b, j: (0, j)),
                  pl.BlockSpec((1, tn), lambda b, j: (0, nj + j))],
        out_specs=pl.BlockSpec((seq, tn), lambda b, j: (b, j)),
        out_shape=jax.ShapeDtypeStruct((m, dff), BF16),
        compiler_params=_params(2),
        name="ffn_up",
    )(x1_bf, w_up, w_up, conv_w, conv_w, cb, cb)


def _ffn_down_kernel(h_ref, w_ref, x_ref, o_ref):
    o_ref[...] = ALPHA * x_ref[...] + jnp.dot(h_ref[...], w_ref[...], preferred_element_type=F32)


def _ffn_down(h, w_down_bf, x1):
    m, k = h.shape
    n = w_down_bf.shape[1]
    tm = _pick(m, (512, 256))
    tn = _pick(n, (512, 256, 128))
    return pl.pallas_call(
        _ffn_down_kernel,
        grid=(n // tn, m // tm),
        in_specs=[pl.BlockSpec((tm, k), lambda j, i: (i, 0)),
                  pl.BlockSpec((k, tn), lambda j, i: (0, j)),
                  pl.BlockSpec((tm, tn), lambda j, i: (i, j))],
        out_specs=pl.BlockSpec((tm, tn), lambda j, i: (i, j)),
        out_shape=jax.ShapeDtypeStruct((m, n), F32),
        compiler_params=_params(2),
        name="ffn_down",
    )(h, w_down_bf, x1)


def kernel(x, w_in, rel_table, g_moba, g_dsa, w_out, ln1_g, ln1_b, w_up, conv_w, conv_b, w_down,
           ln2_g, ln2_b):
    batch, seq, d = x.shape
    m = batch * seq
    nh = d // HEAD_DIM // 2
    wg = nh * HEAD_DIM
    n_main = 6 * wg
    assert w_in.shape == (d, n_main + N_IDX) and seq % TQ == 0 and seq // 4 >= DSA_TOPK

    x2d = x.reshape(m, d)
    x_bf = x2d.astype(BF16)
    w_t = w_in.T

    qk_heads = _proj_qk(x_bf, w_t, wg)
    vt_tiles = _proj_vt(x_bf, w_t, wg)
    qi, ki, wi = _proj_idx(x_bf, w_t, n_main)
    bias = _bias_tiles(rel_table)

    ya = _moba(rel_table, qk_heads, vt_tiles, bias, g_moba.reshape(1, wg), batch, seq, nh,
               0, nh, 0, 0)
    yb = _dsa(rel_table, qk_heads, vt_tiles, bias, qi, ki, wi, g_dsa.reshape(1, wg), batch, seq, nh,
              2 * nh, 3 * nh, nh, nh)

    z1 = _out_proj(ya, yb, w_out, x2d)
    x1, x1_bf = _layer_norm(z1, ln1_g, ln1_b, (F32, BF16))

    hmid = _ffn_up(x1_bf, w_up, conv_w, conv_b, batch, seq)
    z2 = _ffn_down(hmid, w_down.astype(BF16), x1)
    (out,) = _layer_norm(z2, ln2_g, ln2_b, (F32,))
    return out.reshape(batch, seq, d)
```

```python
import functools
import math

import jax
import jax.numpy as jnp
from jax import lax
from jax.experimental import pallas as pl
from jax.experimental.pallas import tpu as pltpu

F32 = jnp.float32
BF16 = jnp.bfloat16
I32 = jnp.int32

HEAD_DIM = 128
MOBA_BLOCK = 256
MOBA_TOPK = 3
IDX_HEADS = 16
IDX_DIM = 64
DSA_TOPK = 256
NUM_BUCKETS = 32
MAX_DISTANCE = 128
CONV_WIDTH = 3
DEPTH = 1
ALPHA = (2.0 * DEPTH) ** 0.25
NEG = -1e30
LN_EPS = 1e-5

TQ = MOBA_BLOCK
LANES = 128
SUBLANES = 8
VMEM_LIMIT = 56 * 1024 * 1024
DSA_VMEM_LIMIT = 61 * 1024 * 1024
HEADS_INTERLEAVED = 16
SCORE_LOOKAHEAD = 4
BF16_SUBLANES = 16
VT_ROWS = HEAD_DIM + BF16_SUBLANES
FFN_ROW_CHUNK = 512

SCALE = HEAD_DIM ** -0.5
INV_SCALE = HEAD_DIM ** 0.5
EXP2_COEF = SCALE * math.log2(math.e)
NT_DIMS = (((1,), (1,)), ((), ()))
INT_MIN = -(2 ** 31)
N_IDX = IDX_HEADS * IDX_DIM + IDX_DIM + IDX_HEADS


def _params(n_axes, vmem=VMEM_LIMIT):
    return pltpu.CompilerParams(dimension_semantics=("arbitrary",) * n_axes,
                                vmem_limit_bytes=vmem)


def _pick(n, candidates):
    for c in candidates:
        if n % c == 0:
            return c
    raise ValueError(f"no tile for {n} among {candidates}")


def _resident(block_shape, index_map):
    return pl.BlockSpec(block_shape, index_map, pipeline_mode=pl.Buffered(1))


def _proj_rows_kernel(x_ref, w_ref, o_ref, wbf_ref):
    @pl.when(pl.program_id(1) == 0)
    def _():
        wbf_ref[...] = w_ref[...].astype(BF16)

    acc = lax.dot_general(x_ref[...], wbf_ref[...], NT_DIMS, preferred_element_type=F32)
    for c in range(o_ref.shape[0]):
        o_ref[c] = acc[:, c * HEAD_DIM:(c + 1) * HEAD_DIM].astype(BF16)


def _proj_cols_kernel(x_ref, w_ref, o_ref, wbf_ref):
    @pl.when(pl.program_id(1) == 0)
    def _():
        wbf_ref[...] = w_ref[...].astype(BF16)

    acc = lax.dot_general(wbf_ref[...], x_ref[...], NT_DIMS, preferred_element_type=F32)
    ones = jnp.ones((VT_ROWS - HEAD_DIM, TQ), BF16)
    for c in range(o_ref.shape[0]):
        for t in range(o_ref.shape[1]):
            o_ref[c, t, :HEAD_DIM, :] = acc[c * HEAD_DIM:(c + 1) * HEAD_DIM, t * TQ:(t + 1) * TQ].astype(BF16)
            o_ref[c, t, HEAD_DIM:, :] = ones


def _proj_qk(x_bf, w_t, wg):
    m, d = x_bf.shape
    tm = _pick(m, (1024, 512, 256))
    tn = _pick(wg, (512, 256, 128))
    tpp = wg // tn
    hp = tn // HEAD_DIM

    def w_map(j, i):
        part = j // tpp
        return ((part + (part >= 2)) * tpp + j % tpp, 0)

    return pl.pallas_call(
        _proj_rows_kernel,
        grid=(4 * tpp, m // tm),
        in_specs=[pl.BlockSpec((tm, d), lambda j, i: (i, 0)),
                  pl.BlockSpec((tn, d), w_map)],
        out_specs=pl.BlockSpec((hp, tm, HEAD_DIM), lambda j, i: (j, i, 0)),
        out_shape=jax.ShapeDtypeStruct((4 * wg // HEAD_DIM, m, HEAD_DIM), BF16),
        scratch_shapes=[pltpu.VMEM((tn, d), BF16)],
        compiler_params=_params(2),
        name="proj_qk",
    )(x_bf, w_t)


def _proj_vt(x_bf, w_t, wg):
    m, d = x_bf.shape
    tm = _pick(m, (1024, 512, 256))
    tn = _pick(wg, (512, 256, 128))
    tpp = wg // tn
    hp = tn // HEAD_DIM
    return pl.pallas_call(
        _proj_cols_kernel,
        grid=(2 * tpp, m // tm),
        in_specs=[pl.BlockSpec((tm, d), lambda j, i: (i, 0)),
                  pl.BlockSpec((tn, d), lambda j, i: ((2 + 3 * (j // tpp)) * tpp + j % tpp, 0))],
        out_specs=pl.BlockSpec((hp, tm // TQ, VT_ROWS, TQ), lambda j, i: (j, i, 0, 0)),
        out_shape=jax.ShapeDtypeStruct((2 * wg // HEAD_DIM, m // TQ, VT_ROWS, TQ), BF16),
        scratch_shapes=[pltpu.VMEM((tn, d), BF16)],
        compiler_params=_params(2),
        name="proj_vt",
    )(x_bf, w_t)


def _proj_idx_kernel(x_ref, wq_ref, wk_ref, ww_ref, qi_ref, ki_ref, wi_ref):
    x = x_ref[...]
    qi = lax.dot_general(x, wq_ref[...].astype(BF16), NT_DIMS, preferred_element_type=F32)
    qi_ref[...] = (qi * IDX_DIM ** -0.5).astype(BF16)
    wk = wk_ref[...].astype(BF16)
    ww = ww_ref[...].astype(BF16)
    pad = jnp.zeros((LANES - IDX_DIM - IDX_HEADS, wk.shape[1]), BF16)
    kw = lax.dot_general(x, jnp.concatenate([wk, ww, pad], axis=0), NT_DIMS, preferred_element_type=F32)
    lane = lax.broadcasted_iota(I32, kw.shape, 1)
    k_lo = jnp.where(lane < IDX_DIM, kw, 0.0)
    ki_ref[0] = k_lo.astype(BF16)
    ki_ref[1] = pltpu.roll(k_lo, IDX_DIM, axis=1).astype(BF16)
    wi_ref[...] = kw.T[IDX_DIM:IDX_DIM + IDX_HEADS] * IDX_HEADS ** -0.5


def _proj_idx(x_bf, w_t, n_main):
    m, d = x_bf.shape
    tm = _pick(m, (1024, 512, 256))
    nq = IDX_HEADS * IDX_DIM
    assert n_main % nq == 0 and (n_main + nq) % IDX_DIM == 0 and (n_main + nq + IDX_DIM) % IDX_HEADS == 0
    return pl.pallas_call(
        _proj_idx_kernel,
        grid=(m // tm,),
        in_specs=[pl.BlockSpec((tm, d), lambda i: (i, 0)),
                  _resident((nq, d), lambda i: (n_main // nq, 0)),
                  _resident((IDX_DIM, d), lambda i: ((n_main + nq) // IDX_DIM, 0)),
                  _resident((IDX_HEADS, d), lambda i: ((n_main + nq + IDX_DIM) // IDX_HEADS, 0))],
        out_specs=[pl.BlockSpec((tm, nq), lambda i: (i, 0)),
                   pl.BlockSpec((2, tm, 2 * IDX_DIM), lambda i: (0, i, 0)),
                   pl.BlockSpec((IDX_HEADS, tm), lambda i: (0, i))],
        out_shape=[jax.ShapeDtypeStruct((m, nq), BF16),
                   jax.ShapeDtypeStruct((2, m, 2 * IDX_DIM), BF16),
                   jax.ShapeDtypeStruct((IDX_HEADS, m), F32)],
        compiler_params=_params(1),
        name="proj_idx",
    )(x_bf, w_t, w_t, w_t)


def _bias_kernel(tab_ref, o_ref):
    h = pl.program_id(0)
    s = lax.broadcasted_iota(I32, (TQ, TQ), 0)
    t = lax.broadcasted_iota(I32, (TQ, TQ), 1)
    max_exact = NUM_BUCKETS // 2
    for slot in range(2):
        dist = (1 - slot) * TQ + t - s
        d = jnp.maximum(dist, 0)
        df = jnp.maximum(d, max_exact).astype(F32)
        large = max_exact + (jnp.log(df / max_exact) / math.log(MAX_DISTANCE / max_exact)
                             * (NUM_BUCKETS - max_exact)).astype(I32)
        large = jnp.minimum(large, NUM_BUCKETS - 1)
        bucket = jnp.where(d < max_exact, d, large)
        out = jnp.zeros((TQ, TQ), F32)
        for k in range(NUM_BUCKETS):
            out = jnp.where(bucket == k, tab_ref[k, h], out)
        o_ref[0, slot] = jnp.where(dist >= 0, out * INV_SCALE, NEG)


def _bias_tiles(rel_table):
    nh = rel_table.shape[1]
    return pl.pallas_call(
        _bias_kernel,
        grid=(nh,),
        in_specs=[pl.BlockSpec(memory_space=pltpu.SMEM)],
        out_specs=pl.BlockSpec((1, 2, TQ, TQ), lambda h: (h, 0, 0, 0)),
        out_shape=jax.ShapeDtypeStruct((nh, 2, TQ, TQ), F32),
        compiler_params=_params(1),
        name="bias_tiles",
    )(rel_table)


def _softmax_step(state, u, row_term, vt):
    tile_max = jnp.max(u, axis=0, keepdims=True)
    if row_term is not None:
        tile_max = tile_max + row_term
    m_new = tile_max if state is None else jnp.maximum(state[0], tile_max)
    off = m_new if row_term is None else m_new - row_term
    p = jnp.exp2((u - off) * EXP2_COEF)
    pv = jnp.dot(vt, p.astype(BF16), preferred_element_type=F32)
    if state is None:
        return m_new, pv
    m_old, acc_old = state
    alpha = jnp.exp2((m_old - m_new) * EXP2_COEF)
    return m_new, alpha * acc_old + pv


def _pipelined(n_slots, issue, consume):
    pending = {g: issue(g) for g in range(min(SCORE_LOOKAHEAD, n_slots))}
    out = []
    for g in range(n_slots):
        if g + SCORE_LOOKAHEAD < n_slots:
            pending[g + SCORE_LOOKAHEAD] = issue(g + SCORE_LOOKAHEAD)
        out.append(consume(g, pending.pop(g)))
    return tuple(out)


def _attend_heads(i, heads, q_ref, k_ref, vt_ref, bias_ref, far_bias, yt_ref,
                  elem_mask=None, row_mask=None):
    def logits(h, n):
        kt = k_ref[h, pl.ds(pl.multiple_of(n * TQ, TQ), TQ), :]
        u = lax.dot_general(kt, q_ref[h], NT_DIMS, preferred_element_type=F32)
        return u if elem_mask is None else u + elem_mask(n)

    ns = len(heads)

    states = _pipelined(
        ns, lambda g: logits(heads[g], i) + bias_ref[heads[g], 1],
        lambda g, u: _softmax_step(None, u, None, vt_ref[heads[g], i]))

    def adjacent(states):
        def consume(g, u):
            rt = None if row_mask is None else row_mask(g, i - 1)
            return _softmax_step(states[g], u, rt, vt_ref[heads[g], i - 1])
        return _pipelined(ns, lambda g: logits(heads[g], i - 1) + bias_ref[heads[g], 0], consume)

    states = lax.cond(i >= 1, adjacent, lambda s: s, states)

    def far_body(n, states):
        def consume(g, u):
            fb = far_bias(heads[g])
            rt = fb if row_mask is None else fb + row_mask(g, n)
            return _softmax_step(states[g], u, rt, vt_ref[heads[g], n])
        return _pipelined(ns, lambda g: logits(heads[g], n), consume)

    states = lax.fori_loop(0, i - 1, far_body, states)

    ss = jnp.zeros((1, TQ), F32)
    for (m, acc), h in zip(states, heads):
        yt = acc[:HEAD_DIM] / acc[HEAD_DIM:HEAD_DIM + 1]
        yt_ref[h] = yt
        ss = ss + jnp.sum(yt * yt, axis=0, keepdims=True)
    return ss


def _group_rms_store(yt_ref, ss, g_ref, o_ref):
    nh = yt_ref.shape[0]
    inv = lax.rsqrt(ss / (nh * HEAD_DIM) + LN_EPS)
    for h in range(nh):
        sl = slice(h * HEAD_DIM, (h + 1) * HEAD_DIM)
        o_ref[:, sl] = ((yt_ref[h] * inv).T * g_ref[:, sl]).astype(BF16)


def _attn_scratch(nh):
    return [pltpu.VMEM((nh, HEAD_DIM, TQ), F32)]


def _attn_in_specs(nh, seq, q_base, k_base, v_base, bias_base, prefetch_kv):
    nb = seq // TQ
    assert q_base % nh == 0 and k_base % nh == 0 and v_base % nh == 0 and bias_base % nh == 0
    kv_spec = pl.BlockSpec if prefetch_kv else _resident
    return [pl.BlockSpec(memory_space=pltpu.SMEM),
            pl.BlockSpec((nh, TQ, HEAD_DIM), lambda b, i: (q_base // nh, b * nb + i, 0)),
            kv_spec((nh, seq, HEAD_DIM), lambda b, i: (k_base // nh, b, 0)),
            kv_spec((nh, nb, VT_ROWS, TQ), lambda b, i: (v_base // nh, b, 0, 0)),
            _resident((nh, 2, TQ, TQ), lambda b, i: (bias_base // nh, 0, 0, 0))]


def _moba_kernel(head0, tab_ref, q_ref, k_ref, vt_ref, bias_ref, g_ref, o_ref,
                 kbar_ref, msel_ref, yt_ref):
    i = pl.program_id(1)
    nh = q_ref.shape[0]
    nb = k_ref.shape[1] // TQ
    hg = msel_ref.shape[0]

    @pl.when(i == 0)
    def _():
        def kbar_body(h, carry):
            for n in range(nb):
                blk = k_ref[h, n * TQ:(n + 1) * TQ, :].astype(F32)
                kbar_ref[h, n:n + 1, :] = jnp.mean(blk, axis=0, keepdims=True)
            return carry
        lax.fori_loop(0, nh, kbar_body, 0)

    blk_idx = lax.broadcasted_iota(I32, (nb, TQ), 0)
    past = blk_idx < i

    def group_body(grp, ss):
        heads = [grp * hg + g for g in range(hg)]
        for g, h in enumerate(heads):
            kb = kbar_ref[h]
            hi = kb.astype(BF16)
            lo = (kb - hi.astype(F32)).astype(BF16)
            both = lax.dot_general(jnp.concatenate([hi, lo], axis=0), q_ref[h], NT_DIMS,
                                   preferred_element_type=F32)
            gate = both[:nb] + both[nb:]
            gate = jnp.where(past, gate, NEG)
            rank = jnp.zeros((nb, TQ), F32)
            for mm in range(nb):
                row = gate[mm:mm + 1, :]
                rank = rank + jnp.where(blk_idx > mm, jnp.where(row >= gate, 1.0, 0.0),
                                        jnp.where(row > gate, 1.0, 0.0))
            keep = jnp.logical_and(past, rank < MOBA_TOPK)
            msel = jnp.where(keep, 0.0, NEG)
            for n in range(nb):
                msel_ref[g, n] = msel[n:n + 1, :]
        ss_g = _attend_heads(i, heads, q_ref, k_ref, vt_ref, bias_ref,
                             lambda h: tab_ref[NUM_BUCKETS - 1, head0 + h] * INV_SCALE, yt_ref,
                             row_mask=lambda g, n: msel_ref[g, n])
        return ss + ss_g

    ss = lax.fori_loop(0, nh // hg, group_body, jnp.zeros((1, TQ), F32))
    _group_rms_store(yt_ref, ss, g_ref, o_ref)


def _moba(rel_table, qk_heads, vt_tiles, bias, g, batch, seq, nh, q_base, k_base, v_base, bias_base):
    nb = seq // TQ
    hg = min(HEADS_INTERLEAVED, nh)
    return pl.pallas_call(
        functools.partial(_moba_kernel, bias_base),
        grid=(batch, nb),
        in_specs=_attn_in_specs(nh, seq, q_base, k_base, v_base, bias_base, prefetch_kv=True)
        + [pl.BlockSpec((1, nh * HEAD_DIM), lambda b, i: (0, 0))],
        out_specs=pl.BlockSpec((TQ, nh * HEAD_DIM), lambda b, i: (b * nb + i, 0)),
        out_shape=jax.ShapeDtypeStruct((batch * seq, nh * HEAD_DIM), BF16),
        scratch_shapes=[pltpu.VMEM((nh, nb, HEAD_DIM), F32),
                        pltpu.VMEM((hg, nb, 1, TQ), F32)] + _attn_scratch(nh),
        compiler_params=_params(2),
        name="moba_attn",
    )(rel_table, qk_heads, qk_heads, vt_tiles, bias, g)


def _sort_key(x):
    bits = pltpu.bitcast(x, I32)
    return jnp.where(bits >= 0, bits, bits ^ jnp.int32(0x7FFFFFFF))


def _dsa_kernel(head0, tab_ref, q_ref, k_ref, vt_ref, bias_ref, qi_ref, ki_ref, wi_ref, g_ref, o_ref,
                key_ref, mask_ref, yt_ref):
    i = pl.program_id(1)
    nh = q_ref.shape[0]
    hg = min(HEADS_INTERLEAVED, nh)
    s_idx = lax.broadcasted_iota(I32, (TQ, TQ), 0)
    t_idx = lax.broadcasted_iota(I32, (TQ, TQ), 1)
    neg_key = _sort_key(jnp.full((TQ, TQ), NEG, F32))

    def score_body(n, carry):
        off = pl.multiple_of(n * TQ, TQ)
        sc = jnp.zeros((TQ, TQ), F32)
        for hh in range(IDX_HEADS):
            kin = ki_ref[hh % 2, pl.ds(off, TQ), :]
            qh = qi_ref[:, (hh // 2) * 2 * IDX_DIM:(hh // 2 + 1) * 2 * IDX_DIM]
            s = lax.dot_general(kin, qh, NT_DIMS, preferred_element_type=F32)
            sc = sc + jnp.maximum(s, 0.0) * wi_ref[hh:hh + 1, :]
        sc = sc + 0.0
        admissible = jnp.logical_or(n < i, s_idx <= t_idx)
        key_ref[n] = jnp.where(admissible, _sort_key(sc), neg_key)
        return carry

    lax.fori_loop(0, i + 1, score_body, 0)

    def count_ge(cand):
        def cnt_body(n, part):
            c = jnp.where(key_ref[n] >= cand, 1.0, 0.0)
            c = c.reshape(4, TQ // SUBLANES // 4, SUBLANES, TQ)
            return part + jnp.sum(jnp.sum(c, axis=1), axis=0)
        part = lax.fori_loop(0, i + 1, cnt_body, jnp.zeros((SUBLANES, TQ), F32))
        return jnp.sum(part, axis=0, keepdims=True)

    def bisect(it, thr):
        cand = thr + jnp.left_shift(jnp.int32(1), 31 - it)
        return jnp.where(count_ge(cand) >= DSA_TOPK, cand, thr)

    thr = lax.fori_loop(0, 32, bisect, jnp.full((1, TQ), INT_MIN, I32))
    ties_wanted = DSA_TOPK - count_ge(thr + 1)
    lower = jnp.where(t_idx < s_idx, 1.0, 0.0).astype(BF16)

    def mask_body(n, seen):
        kk = key_ref[n]
        eq = kk == thr
        eqf = jnp.where(eq, 1.0, 0.0)
        before = jnp.dot(lower, eqf.astype(BF16), preferred_element_type=F32) + seen
        keep_tie = jnp.where(before < ties_wanted, 0.0, NEG)
        mask_ref[n] = jnp.where(kk > thr, 0.0, jnp.where(eq, keep_tie, NEG))
        return seen + jnp.sum(eqf, axis=0, keepdims=True)

    lax.fori_loop(0, i + 1, mask_body, jnp.zeros((1, TQ), F32))

    def group_body(grp, ss):
        heads = [grp * hg + g for g in range(hg)]
        ss_g = _attend_heads(i, heads, q_ref, k_ref, vt_ref, bias_ref,
                             lambda h: tab_ref[NUM_BUCKETS - 1, head0 + h] * INV_SCALE, yt_ref,
                             elem_mask=lambda n: mask_ref[n])
        return ss + ss_g

    ss = lax.fori_loop(0, nh // hg, group_body, jnp.zeros((1, TQ), F32))
    _group_rms_store(yt_ref, ss, g_ref, o_ref)


def _dsa(rel_table, qk_heads, vt_tiles, bias, qi, ki, wi, g, batch, seq, nh,
         q_base, k_base, v_base, bias_base):
    nb = seq // TQ
    return pl.pallas_call(
        functools.partial(_dsa_kernel, bias_base),
        grid=(batch, nb),
        in_specs=_attn_in_specs(nh, seq, q_base, k_base, v_base, bias_base, prefetch_kv=True)
        + [pl.BlockSpec((TQ, IDX_HEADS * IDX_DIM), lambda b, i: (b * nb + i, 0)),
           _resident((2, seq, 2 * IDX_DIM), lambda b, i: (0, b, 0)),
           pl.BlockSpec((IDX_HEADS, TQ), lambda b, i: (0, b * nb + i)),
           pl.BlockSpec((1, nh * HEAD_DIM), lambda b, i: (0, 0))],
        out_specs=pl.BlockSpec((TQ, nh * HEAD_DIM), lambda b, i: (b * nb + i, 0)),
        out_shape=jax.ShapeDtypeStruct((batch * seq, nh * HEAD_DIM), BF16),
        scratch_shapes=[pltpu.VMEM((nb, TQ, TQ), I32),
                        pltpu.VMEM((nb, TQ, TQ), F32)] + _attn_scratch(nh),
        compiler_params=_params(2, vmem=DSA_VMEM_LIMIT),
        name="dsa_attn",
    )(rel_table, qk_heads, qk_heads, vt_tiles, bias, qi, ki, wi, g)


def _out_proj_kernel(a_ref, b_ref, wa_ref, wb_ref, x_ref, o_ref, wbf_ref):
    ka = a_ref.shape[1]

    @pl.when(pl.program_id(1) == 0)
    def _():
        wbf_ref[:ka, :] = wa_ref[...].astype(BF16)
        wbf_ref[ka:, :] = wb_ref[...].astype(BF16)

    acc = jnp.dot(a_ref[...], wbf_ref[:ka, :], preferred_element_type=F32)
    acc = acc + jnp.dot(b_ref[...], wbf_ref[ka:, :], preferred_element_type=F32)
    o_ref[...] = ALPHA * x_ref[...] + acc


def _out_proj(ya, yb, w_out, x2d):
    m, ka = ya.shape
    kb = yb.shape[1]
    assert ka == kb
    n = w_out.shape[1]
    tm = _pick(m, (1024, 512, 256))
    tn = _pick(n, (512, 256, 128))
    return pl.pallas_call(
        _out_proj_kernel,
        grid=(n // tn, m // tm),
        in_specs=[pl.BlockSpec((tm, ka), lambda j, i: (i, 0)),
                  pl.BlockSpec((tm, kb), lambda j, i: (i, 0)),
                  pl.BlockSpec((ka, tn), lambda j, i: (0, j)),
                  pl.BlockSpec((kb, tn), lambda j, i: (1, j)),
                  pl.BlockSpec((tm, tn), lambda j, i: (i, j))],
        out_specs=pl.BlockSpec((tm, tn), lambda j, i: (i, j)),
        out_shape=jax.ShapeDtypeStruct((m, n), F32),
        scratch_shapes=[pltpu.VMEM((ka + kb, tn), BF16)],
        compiler_params=_params(2),
        name="out_proj",
    )(ya, yb, w_out, w_out, x2d)


def _ln_kernel(z_ref, g_ref, b_ref, *o_refs):
    z = z_ref[...]
    mu = jnp.mean(z, axis=1, keepdims=True)
    zc = z - mu
    var = jnp.mean(zc * zc, axis=1, keepdims=True)
    y = zc * lax.rsqrt(var + LN_EPS) * g_ref[...] + b_ref[...]
    for o_ref in o_refs:
        o_ref[...] = y.astype(o_ref.dtype)


def _layer_norm(z, g, b, out_dtypes):
    m, d = z.shape
    tm = _pick(m, (512, 256, 128))
    row = pl.BlockSpec((tm, d), lambda i: (i, 0))
    vec = pl.BlockSpec((1, d), lambda i: (0, 0))
    return pl.pallas_call(
        _ln_kernel,
        grid=(m // tm,),
        in_specs=[row, vec, vec],
        out_specs=[row for _ in out_dtypes],
        out_shape=[jax.ShapeDtypeStruct((m, d), dt) for dt in out_dtypes],
        compiler_params=_params(1),
        name="layer_norm",
    )(z, g.reshape(1, d), b.reshape(1, d))


def _causal_conv(u, history, cw_ref, cb_ref):
    ext = jnp.concatenate([history, u], axis=0)
    out = None
    for j in range(CONV_WIDTH):
        back = CONV_WIDTH - 1 - j
        uj = u if back == 0 else pltpu.roll(ext, back, axis=0)[SUBLANES:]
        term = cw_ref[j:j + 1, :] * uj
        out = term if out is None else out + term
    return out + cb_ref[...]


def _ffn_up_kernel(x_ref, wv_ref, wg_ref, cwv_ref, cwg_ref, cbv_ref, cbg_ref, o_ref):
    wv = wv_ref[...].astype(BF16)
    wg = wg_ref[...].astype(BF16)
    n_chunks = x_ref.shape[0] // FFN_ROW_CHUNK

    def matmuls(c):
        x = x_ref[c * FFN_ROW_CHUNK:(c + 1) * FFN_ROW_CHUNK, :]
        return (jnp.dot(x, wv, preferred_element_type=F32), jnp.dot(x, wg, preferred_element_type=F32))

    pending = matmuls(0)
    history = (jnp.zeros((SUBLANES, wv.shape[1]), F32),) * 2
    for c in range(n_chunks):
        uv, ug = pending
        if c + 1 < n_chunks:
            pending = matmuls(c + 1)
        val = _causal_conv(uv, history[0], cwv_ref, cbv_ref)
        gate = _causal_conv(ug, history[1], cwg_ref, cbg_ref)
        o_ref[c * FFN_ROW_CHUNK:(c + 1) * FFN_ROW_CHUNK, :] = (
            val * (gate * (1.0 / (1.0 + jnp.exp(-gate))))).astype(BF16)
        history = (uv[-SUBLANES:], ug[-SUBLANES:])


def _ffn_up(x1_bf, w_up, conv_w, conv_b, batch, seq):
    m, d = x1_bf.shape
    dff = w_up.shape[1] // 2
    tn = _pick(dff, (256, 128))
    nj = dff // tn
    cb = conv_b.reshape(1, 2 * dff)
    return pl.pallas_call(
        _ffn_up_kernel,
        grid=(batch, nj),
        in_specs=[_resident((seq, d), lambda b, j: (b, 0)),
                  pl.BlockSpec((d, tn), lambda b, j: (0, j)),
                  pl.BlockSpec((d, tn), lambda b, j: (0, nj + j)),
                  pl.BlockSpec((CONV_WIDTH, tn), lambda b, j: (0, j)),
                  pl.BlockSpec((CONV_WIDTH, tn), lambda b, j: (0, nj + j)),
                  pl.BlockSpec((1, tn), lambda b, j: (0, j)),
                  pl.BlockSpec((1, tn), lambda b, j: (0, nj + j))],
        out_specs=pl.BlockSpec((seq, tn), lambda b, j: (b, j)),
        out_shape=jax.ShapeDtypeStruct((m, dff), BF16),
        compiler_params=_params(2),
        name="ffn_up",
    )(x1_bf, w_up, w_up, conv_w, conv_w, cb, cb)


def _ffn_down_kernel(h_ref, w_ref, x_ref, o_ref):
    o_ref[...] = ALPHA * x_ref[...] + jnp.dot(h_ref[...], w_ref[...], preferred_element_type=F32)


def _ffn_down(h, w_down_bf, x1):
    m, k = h.shape
    n = w_down_bf.shape[1]
    tm = _pick(m, (512, 256))
    tn = _pick(n, (512, 256, 128))
    return pl.pallas_call(
        _ffn_down_kernel,
        grid=(n // tn, m // tm),
        in_specs=[pl.BlockSpec((tm, k), lambda j, i: (i, 0)),
                  pl.BlockSpec((k, tn), lambda j, i: (0, j)),
                  pl.BlockSpec((tm, tn), lambda j, i: (i, j))],
        out_specs=pl.BlockSpec((tm, tn), lambda j, i: (i, j)),
        out_shape=jax.ShapeDtypeStruct((m, n), F32),
        compiler_params=_params(2),
        name="ffn_down",
    )(h, w_down_bf, x1)


def kernel(x, w_in, rel_table, g_moba, g_dsa, w_out, ln1_g, ln1_b, w_up, conv_w, conv_b, w_down,
           ln2_g, ln2_b):
    batch, seq, d = x.shape
    m = batch * seq
    nh = d // HEAD_DIM // 2
    wg = nh * HEAD_DIM
    n_main = 6 * wg
    assert w_in.shape == (d, n_main + N_IDX) and seq % TQ == 0 and seq // 4 >= DSA_TOPK

    x2d = x.reshape(m, d)
    x_bf = x2d.astype(BF16)
    w_t = w_in.T

    qk_heads = _proj_qk(x_bf, w_t, wg)
    vt_tiles = _proj_vt(x_bf, w_t, wg)
    qi, ki, wi = _proj_idx(x_bf, w_t, n_main)
    bias = _bias_tiles(rel_table)

    ya = _moba(rel_table, qk_heads, vt_tiles, bias, g_moba.reshape(1, wg), batch, seq, nh,
               0, nh, 0, 0)
    yb = _dsa(rel_table, qk_heads, vt_tiles, bias, qi, ki, wi, g_dsa.reshape(1, wg), batch, seq, nh,
              2 * nh, 3 * nh, nh, nh)

    z1 = _out_proj(ya, yb, w_out, x2d)
    x1, x1_bf = _layer_norm(z1, ln1_g, ln1_b, (F32, BF16))

    hmid = _ffn_up(x1_bf, w_up, conv_w, conv_b, batch, seq)
    z2 = _ffn_down(hmid, w_down.astype(BF16), x1)
    (out,) = _layer_norm(z2, ln2_g, ln2_b, (F32,))
    return out.reshape(batch, seq, d)
```

```python
import functools
import math

import jax
import jax.numpy as jnp
from jax import lax
from jax.experimental import pallas as pl
from jax.experimental.pallas import tpu as pltpu

F32 = jnp.float32
BF16 = jnp.bfloat16
I32 = jnp.int32

HEAD_DIM = 128
MOBA_BLOCK = 256
MOBA_TOPK = 3
IDX_HEADS = 16
IDX_DIM = 64
DSA_TOPK = 256
NUM_BUCKETS = 32
MAX_DISTANCE = 128
CONV_WIDTH = 3
DEPTH = 1
ALPHA = (2.0 * DEPTH) ** 0.25
NEG = -1e30
LN_EPS = 1e-5

TQ = MOBA_BLOCK
LANES = 128
SUBLANES = 8
VMEM_LIMIT = 56 * 1024 * 1024
DSA_VMEM_LIMIT = 61 * 1024 * 1024
HEADS_INTERLEAVED = 16
SCORE_LOOKAHEAD = 4
BF16_SUBLANES = 16
VT_ROWS = HEAD_DIM + BF16_SUBLANES
FFN_ROW_CHUNK = 512

SCALE = HEAD_DIM ** -0.5
INV_SCALE = HEAD_DIM ** 0.5
EXP2_COEF = SCALE * math.log2(math.e)
NT_DIMS = (((1,), (1,)), ((), ()))
INT_MIN = -(2 ** 31)
N_IDX = IDX_HEADS * IDX_DIM + IDX_DIM + IDX_HEADS


def _params(n_axes, vmem=VMEM_LIMIT):
    return pltpu.CompilerParams(dimension_semantics=("arbitrary",) * n_axes,
                                vmem_limit_bytes=vmem)


def _pick(n, candidates):
    for c in candidates:
        if n % c == 0:
            return c
    raise ValueError(f"no tile for {n} among {candidates}")


def _resident(block_shape, index_map):
    return pl.BlockSpec(block_shape, index_map, pipeline_mode=pl.Buffered(1))


def _proj_rows_kernel(x_ref, w_ref, o_ref, wbf_ref):
    @pl.when(pl.program_id(1) == 0)
    def _():
        wbf_ref[...] = w_ref[...].astype(BF16)

    acc = lax.dot_general(x_ref[...], wbf_ref[...], NT_DIMS, preferred_element_type=F32)
    for c in range(o_ref.shape[0]):
        o_ref[c] = acc[:, c * HEAD_DIM:(c + 1) * HEAD_DIM].astype(BF16)


def _proj_cols_kernel(x_ref, w_ref, o_ref, wbf_ref):
    @pl.when(pl.program_id(1) == 0)
    def _():
        wbf_ref[...] = w_ref[...].astype(BF16)

    acc = lax.dot_general(wbf_ref[...], x_ref[...], NT_DIMS, preferred_element_type=F32)
    ones = jnp.ones((VT_ROWS - HEAD_DIM, TQ), BF16)
    for c in range(o_ref.shape[0]):
        for t in range(o_ref.shape[1]):
            o_ref[c, t, :HEAD_DIM, :] = acc[c * HEAD_DIM:(c + 1) * HEAD_DIM, t * TQ:(t + 1) * TQ].astype(BF16)
            o_ref[c, t, HEAD_DIM:, :] = ones


def _proj_qk(x_bf, w_t, wg):
    m, d = x_bf.shape
    tm = _pick(m, (1024, 512, 256))
    tn = _pick(wg, (512, 256, 128))
    tpp = wg // tn
    hp = tn // HEAD_DIM

    def w_map(j, i):
        part = j // tpp
        return ((part + (part >= 2)) * tpp + j % tpp, 0)

    return pl.pallas_call(
        _proj_rows_kernel,
        grid=(4 * tpp, m // tm),
        in_specs=[pl.BlockSpec((tm, d), lambda j, i: (i, 0)),
                  pl.BlockSpec((tn, d), w_map)],
        out_specs=pl.BlockSpec((hp, tm, HEAD_DIM), lambda j, i: (j, i, 0)),
        out_shape=jax.ShapeDtypeStruct((4 * wg // HEAD_DIM, m, HEAD_DIM), BF16),
        scratch_shapes=[pltpu.VMEM((tn, d), BF16)],
        compiler_params=_params(2),
        name="proj_qk",
    )(x_bf, w_t)


def _proj_vt(x_bf, w_t, wg):
    m, d = x_bf.shape
    tm = _pick(m, (1024, 512, 256))
    tn = _pick(wg, (512, 256, 128))
    tpp = wg // tn
    hp = tn // HEAD_DIM
    return pl.pallas_call(
        _proj_cols_kernel,
        grid=(2 * tpp, m // tm),
        in_specs=[pl.BlockSpec((tm, d), lambda j, i: (i, 0)),
                  pl.BlockSpec((tn, d), lambda j, i: ((2 + 3 * (j // tpp)) * tpp + j % tpp, 0))],
        out_specs=pl.BlockSpec((hp, tm // TQ, VT_ROWS, TQ), lambda j, i: (j, i, 0, 0)),
        out_shape=jax.ShapeDtypeStruct((2 * wg // HEAD_DIM, m // TQ, VT_ROWS, TQ), BF16),
        scratch_shapes=[pltpu.VMEM((tn, d), BF16)],
        compiler_params=_params(2),
        name="proj_vt",
    )(x_bf, w_t)


def _proj_idx_kernel(x_ref, wq_ref, wk_ref, ww_ref, qi_ref, ki_ref, wi_ref):
    x = x_ref[...]
    qi = lax.dot_general(x, wq_ref[...].astype(BF16), NT_DIMS, preferred_element_type=F32)
    qi_ref[...] = (qi * IDX_DIM ** -0.5).astype(BF16)
    wk = wk_ref[...].astype(BF16)
    ww = ww_ref[...].astype(BF16)
    pad = jnp.zeros((LANES - IDX_DIM - IDX_HEADS, wk.shape[1]), BF16)
    kw = lax.dot_general(x, jnp.concatenate([wk, ww, pad], axis=0), NT_DIMS, preferred_element_type=F32)
    lane = lax.broadcasted_iota(I32, kw.shape, 1)
    k_lo = jnp.where(lane < IDX_DIM, kw, 0.0)
    ki_ref[0] = k_lo.astype(BF16)
    ki_ref[1] = pltpu.roll(k_lo, IDX_DIM, axis=1).astype(BF16)
    wi_ref[...] = kw.T[IDX_DIM:IDX_DIM + IDX_HEADS] * IDX_HEADS ** -0.5


def _proj_idx(x_bf, w_t, n_main):
    m, d = x_bf.shape
    tm = _pick(m, (1024, 512, 256))
    nq = IDX_HEADS * IDX_DIM
    assert n_main % nq == 0 and (n_main + nq) % IDX_DIM == 0 and (n_main + nq + IDX_DIM) % IDX_HEADS == 0
    return pl.pallas_call(
        _proj_idx_kernel,
        grid=(m // tm,),
        in_specs=[pl.BlockSpec((tm, d), lambda i: (i, 0)),
                  _resident((nq, d), lambda i: (n_main // nq, 0)),
                  _resident((IDX_DIM, d), lambda i: ((n_main + nq) // IDX_DIM, 0)),
                  _resident((IDX_HEADS, d), lambda i: ((n_main + nq + IDX_DIM) // IDX_HEADS, 0))],
        out_specs=[pl.BlockSpec((tm, nq), lambda i: (i, 0)),
                   pl.BlockSpec((2, tm, 2 * IDX_DIM), lambda i: (0, i, 0)),
                   pl.BlockSpec((IDX_HEADS, tm), lambda i: (0, i))],
        out_shape=[jax.ShapeDtypeStruct((m, nq), BF16),
                   jax.ShapeDtypeStruct((2, m, 2 * IDX_DIM), BF16),
                   jax.ShapeDtypeStruct((IDX_HEADS, m), F32)],
        compiler_params=_params(1),
        name="proj_idx",
    )(x_bf, w_t, w_t, w_t)


def _bias_kernel(tab_ref, o_ref):
    h = pl.program_id(0)
    s = lax.broadcasted_iota(I32, (TQ, TQ), 0)
    t = lax.broadcasted_iota(I32, (TQ, TQ), 1)
    max_exact = NUM_BUCKETS // 2
    for slot in range(2):
        dist = (1 - slot) * TQ + t - s
        d = jnp.maximum(dist, 0)
        df = jnp.maximum(d, max_exact).astype(F32)
        large = max_exact + (jnp.log(df / max_exact) / math.log(MAX_DISTANCE / max_exact)
                             * (NUM_BUCKETS - max_exact)).astype(I32)
        large = jnp.minimum(large, NUM_BUCKETS - 1)
        bucket = jnp.where(d < max_exact, d, large)
        out = jnp.zeros((TQ, TQ), F32)
        for k in range(NUM_BUCKETS):
            out = jnp.where(bucket == k, tab_ref[k, h], out)
        o_ref[0, slot] = jnp.where(dist >= 0, out * INV_SCALE, NEG)


def _bias_tiles(rel_table):
    nh = rel_table.shape[1]
    return pl.pallas_call(
        _bias_kernel,
        grid=(nh,),
        in_specs=[pl.BlockSpec(memory_space=pltpu.SMEM)],
        out_specs=pl.BlockSpec((1, 2, TQ, TQ), lambda h: (h, 0, 0, 0)),
        out_shape=jax.ShapeDtypeStruct((nh, 2, TQ, TQ), F32),
        compiler_params=_params(1),
        name="bias_tiles",
    )(rel_table)


def _softmax_step(state, u, row_term, vt):
    tile_max = jnp.max(u, axis=0, keepdims=True)
    if row_term is not None:
        tile_max = tile_max + row_term
    m_new = tile_max if state is None else jnp.maximum(state[0], tile_max)
    off = m_new if row_term is None else m_new - row_term
    p = jnp.exp2((u - off) * EXP2_COEF)
    pv = jnp.dot(vt, p.astype(BF16), preferred_element_type=F32)
    if state is None:
        return m_new, pv
    m_old, acc_old = state
    alpha = jnp.exp2((m_old - m_new) * EXP2_COEF)
    return m_new, alpha * acc_old + pv


def _pipelined(n_slots, issue, consume):
    pending = {g: issue(g) for g in range(min(SCORE_LOOKAHEAD, n_slots))}
    out = []
    for g in range(n_slots):
        if g + SCORE_LOOKAHEAD < n_slots:
            pending[g + SCORE_LOOKAHEAD] = issue(g + SCORE_LOOKAHEAD)
        out.append(consume(g, pending.pop(g)))
    return tuple(out)


def _attend_heads(i, heads, q_ref, k_ref, vt_ref, bias_ref, far_bias, yt_ref,
                  elem_mask=None, row_mask=None):
    def logits(h, n):
        kt = k_ref[h, pl.ds(pl.multiple_of(n * TQ, TQ), TQ), :]
        u = lax.dot_general(kt, q_ref[h], NT_DIMS, preferred_element_type=F32)
        return u if elem_mask is None else u + elem_mask(n)

    ns = len(heads)

    states = _pipelined(
        ns, lambda g: logits(heads[g], i) + bias_ref[heads[g], 1],
        lambda g, u: _softmax_step(None, u, None, vt_ref[heads[g], i]))

    def adjacent(states):
        def consume(g, u):
            rt = None if row_mask is None else row_mask(g, i - 1)
            return _softmax_step(states[g], u, rt, vt_ref[heads[g], i - 1])
        return _pipelined(ns, lambda g: logits(heads[g], i - 1) + bias_ref[heads[g], 0], consume)

    states = lax.cond(i >= 1, adjacent, lambda s: s, states)

    def far_body(n, states):
        def consume(g, u):
            fb = far_bias(heads[g])
            rt = fb if row_mask is None else fb + row_mask(g, n)
            return _softmax_step(states[g], u, rt, vt_ref[heads[g], n])
        return _pipelined(ns, lambda g: logits(heads[g], n), consume)

    states = lax.fori_loop(0, i - 1, far_body, states)

    ss = jnp.zeros((1, TQ), F32)
    for (m, acc), h in zip(states, heads):
        yt = acc[:HEAD_DIM] / acc[HEAD_DIM:HEAD_DIM + 1]
        yt_ref[h] = yt
        ss = ss + jnp.sum(yt * yt, axis=0, keepdims=True)
    return ss


def _group_rms_store(yt_ref, ss, g_ref, o_ref):
    nh = yt_ref.shape[0]
    inv = lax.rsqrt(ss / (nh * HEAD_DIM) + LN_EPS)
    for h in range(nh):
        sl = slice(h * HEAD_DIM, (h + 1) * HEAD_DIM)
        o_ref[:, sl] = ((yt_ref[h] * inv).T * g_ref[:, sl]).astype(BF16)


def _attn_scratch(nh):
    return [pltpu.VMEM((nh, HEAD_DIM, TQ), F32)]


def _attn_in_specs(nh, seq, q_base, k_base, v_base, bias_base, prefetch_kv):
    nb = seq // TQ
    assert q_base % nh == 0 and k_base % nh == 0 and v_base % nh == 0 and bias_base % nh == 0
    kv_spec = pl.BlockSpec if prefetch_kv else _resident
    return [pl.BlockSpec(memory_space=pltpu.SMEM),
            pl.BlockSpec((nh, TQ, HEAD_DIM), lambda b, i: (q_base // nh, b * nb + i, 0)),
            kv_spec((nh, seq, HEAD_DIM), lambda b, i: (k_base // nh, b, 0)),
            kv_spec((nh, nb, VT_ROWS, TQ), lambda b, i: (v_base // nh, b, 0, 0)),
            _resident((nh, 2, TQ, TQ), lambda b, i: (bias_base // nh, 0, 0, 0))]


def _moba_kernel(head0, tab_ref, q_ref, k_ref, vt_ref, bias_ref, g_ref, o_ref,
                 kbar_ref, msel_ref, yt_ref):
    i = pl.program_id(1)
    nh = q_ref.shape[0]
    nb = k_ref.shape[1] // TQ
    hg = msel_ref.shape[0]

    @pl.when(i == 0)
    def _():
        def kbar_body(h, carry):
            for n in range(nb):
                blk = k_ref[h, n * TQ:(n + 1) * TQ, :].astype(F32)
                kbar_ref[h, n:n + 1, :] = jnp.mean(blk, axis=0, keepdims=True)
            return carry
        lax.fori_loop(0, nh, kbar_body, 0)

    blk_idx = lax.broadcasted_iota(I32, (nb, TQ), 0)
    past = blk_idx < i

    def group_body(grp, ss):
        heads = [grp * hg + g for g in range(hg)]
        for g, h in enumerate(heads):
            kb = kbar_ref[h]
            hi = kb.astype(BF16)
            lo = (kb - hi.astype(F32)).astype(BF16)
            both = lax.dot_general(jnp.concatenate([hi, lo], axis=0), q_ref[h], NT_DIMS,
                                   preferred_element_type=F32)
            gate = both[:nb] + both[nb:]
            gate = jnp.where(past, gate, NEG)
            rank = jnp.zeros((nb, TQ), F32)
            for mm in range(nb):
                row = gate[mm:mm + 1, :]
                rank = rank + jnp.where(blk_idx > mm, jnp.where(row >= gate, 1.0, 0.0),
                                        jnp.where(row > gate, 1.0, 0.0))
            keep = jnp.logical_and(past, rank < MOBA_TOPK)
            msel = jnp.where(keep, 0.0, NEG)
            for n in range(nb):
                msel_ref[g, n] = msel[n:n + 1, :]
        ss_g = _attend_heads(i, heads, q_ref, k_ref, vt_ref, bias_ref,
                             lambda h: tab_ref[NUM_BUCKETS - 1, head0 + h] * INV_SCALE, yt_ref,
                             row_mask=lambda g, n: msel_ref[g, n])
        return ss + ss_g

    ss = lax.fori_loop(0, nh // hg, group_body, jnp.zeros((1, TQ), F32))
    _group_rms_store(yt_ref, ss, g_ref, o_ref)


def _moba(rel_table, qk_heads, vt_tiles, bias, g, batch, seq, nh, q_base, k_base, v_base, bias_base):
    nb = seq // TQ
    hg = min(HEADS_INTERLEAVED, nh)
    return pl.pallas_call(
        functools.partial(_moba_kernel, bias_base),
        grid=(batch, nb),
        in_specs=_attn_in_specs(nh, seq, q_base, k_base, v_base, bias_base, prefetch_kv=True)
        + [pl.BlockSpec((1, nh * HEAD_DIM), lambda b, i: (0, 0))],
        out_specs=pl.BlockSpec((TQ, nh * HEAD_DIM), lambda b, i: (b * nb + i, 0)),
        out_shape=jax.ShapeDtypeStruct((batch * seq, nh * HEAD_DIM), BF16),
        scratch_shapes=[pltpu.VMEM((nh, nb, HEAD_DIM), F32),
                        pltpu.VMEM((hg, nb, 1, TQ), F32)] + _attn_scratch(nh),
        compiler_params=_params(2),
        name="moba_attn",
    )(rel_table, qk_heads, qk_heads, vt_tiles, bias, g)


def _sort_key(x):
    bits = pltpu.bitcast(x, I32)
    return jnp.where(bits >= 0, bits, bits ^ jnp.int32(0x7FFFFFFF))


def _dsa_kernel(head0, tab_ref, q_ref, k_ref, vt_ref, bias_ref, qi_ref, ki_ref, wi_ref, g_ref, o_ref,
                key_ref, mask_ref, yt_ref):
    i = pl.program_id(1)
    nh = q_ref.shape[0]
    hg = min(HEADS_INTERLEAVED, nh)
    s_idx = lax.broadcasted_iota(I32, (TQ, TQ), 0)
    t_idx = lax.broadcasted_iota(I32, (TQ, TQ), 1)
    neg_key = _sort_key(jnp.full((TQ, TQ), NEG, F32))

    def score_body(n, carry):
        off = pl.multiple_of(n * TQ, TQ)
        sc = jnp.zeros((TQ, TQ), F32)
        for hh in range(IDX_HEADS):
            kin = ki_ref[hh % 2, pl.ds(off, TQ), :]
            qh = qi_ref[:, (hh // 2) * 2 * IDX_DIM:(hh // 2 + 1) * 2 * IDX_DIM]
            s = lax.dot_general(kin, qh, NT_DIMS, preferred_element_type=F32)
            sc = sc + jnp.maximum(s, 0.0) * wi_ref[hh:hh + 1, :]
        sc = sc + 0.0
        admissible = jnp.logical_or(n < i, s_idx <= t_idx)
        key_ref[n] = jnp.where(admissible, _sort_key(sc), neg_key)
        return carry

    lax.fori_loop(0, i + 1, score_body, 0)

    def count_ge(cand):
        def cnt_body(n, part):
            c = jnp.where(key_ref[n] >= cand, 1.0, 0.0)
            c = c.reshape(4, TQ // SUBLANES // 4, SUBLANES, TQ)
            return part + jnp.sum(jnp.sum(c, axis=1), axis=0)
        part = lax.fori_loop(0, i + 1, cnt_body, jnp.zeros((SUBLANES, TQ), F32))
        return jnp.sum(part, axis=0, keepdims=True)

    def bisect(it, thr):
        cand = thr + jnp.left_shift(jnp.int32(1), 31 - it)
        return jnp.where(count_ge(cand) >= DSA_TOPK, cand, thr)

    thr = lax.fori_loop(0, 32, bisect, jnp.full((1, TQ), INT_MIN, I32))
    ties_wanted = DSA_TOPK - count_ge(thr + 1)
    lower = jnp.where(t_idx < s_idx, 1.0, 0.0).astype(BF16)

    def mask_body(n, seen):
        kk = key_ref[n]
        eq = kk == thr
        eqf = jnp.where(eq, 1.0, 0.0)
        before = jnp.dot(lower, eqf.astype(BF16), preferred_element_type=F32) + seen
        keep_tie = jnp.where(before < ties_wanted, 0.0, NEG)
        mask_ref[n] = jnp.where(kk > thr, 0.0, jnp.where(eq, keep_tie, NEG))
        return seen + jnp.sum(eqf, axis=0, keepdims=True)

    lax.fori_loop(0, i + 1, mask_body, jnp.zeros((1, TQ), F32))

    def group_body(grp, ss):
        heads = [grp * hg + g for g in range(hg)]
        ss_g = _attend_heads(i, heads, q_ref, k_ref, vt_ref, bias_ref,
                             lambda h: tab_ref[NUM_BUCKETS - 1, head0 + h] * INV_SCALE, yt_ref,
                             elem_mask=lambda n: mask_ref[n])
        return ss + ss_g

    ss = lax.fori_loop(0, nh // hg, group_body, jnp.zeros((1, TQ), F32))
    _group_rms_store(yt_ref, ss, g_ref, o_ref)


def _dsa(rel_table, qk_heads, vt_tiles, bias, qi, ki, wi, g, batch, seq, nh,
         q_base, k_base, v_base, bias_base):
    nb = seq // TQ
    return pl.pallas_call(
        functools.partial(_dsa_kernel, bias_base),
        grid=(batch, nb),
        in_specs=_attn_in_specs(nh, seq, q_base, k_base, v_base, bias_base, prefetch_kv=True)
        + [pl.BlockSpec((TQ, IDX_HEADS * IDX_DIM), lambda b, i: (b * nb + i, 0)),
           _resident((2, seq, 2 * IDX_DIM), lambda b, i: (0, b, 0)),
           pl.BlockSpec((IDX_HEADS, TQ), lambda b, i: (0, b * nb + i)),
           pl.BlockSpec((1, nh * HEAD_DIM), lambda b, i: (0, 0))],
        out_specs=pl.BlockSpec((TQ, nh * HEAD_DIM), lambda b, i: (b * nb + i, 0)),
        out_shape=jax.ShapeDtypeStruct((batch * seq, nh * HEAD_DIM), BF16),
        scratch_shapes=[pltpu.VMEM((nb, TQ, TQ), I32),
                        pltpu.VMEM((nb, TQ, TQ), F32)] + _attn_scratch(nh),
        compiler_params=_params(2, vmem=DSA_VMEM_LIMIT),
        name="dsa_attn",
    )(rel_table, qk_heads, qk_heads, vt_tiles, bias, qi, ki, wi, g)


def _out_proj_kernel(a_ref, b_ref, wa_ref, wb_ref, x_ref, o_ref, wbf_ref):
    ka = a_ref.shape[1]

    @pl.when(pl.program_id(1) == 0)
    def _():
        wbf_ref[:ka, :] = wa_ref[...].astype(BF16)
        wbf_ref[ka:, :] = wb_ref[...].astype(BF16)

    acc = jnp.dot(a_ref[...], wbf_ref[:ka, :], preferred_element_type=F32)
    acc = acc + jnp.dot(b_ref[...], wbf_ref[ka:, :], preferred_element_type=F32)
    o_ref[...] = ALPHA * x_ref[...] + acc


def _out_proj(ya, yb, w_out, x2d):
    m, ka = ya.shape
    kb = yb.shape[1]
    assert ka == kb
    n = w_out.shape[1]
    tm = _pick(m, (1024, 512, 256))
    tn = _pick(n, (512, 256, 128))
    return pl.pallas_call(
        _out_proj_kernel,
        grid=(n // tn, m // tm),
        in_specs=[pl.BlockSpec((tm, ka), lambda j, i: (i, 0)),
                  pl.BlockSpec((tm, kb), lambda j, i: (i, 0)),
                  pl.BlockSpec((ka, tn), lambda j, i: (0, j)),
                  pl.BlockSpec((kb, tn), lambda j, i: (1, j)),
                  pl.BlockSpec((tm, tn), lambda j, i: (i, j))],
        out_specs=pl.BlockSpec((tm, tn), lambda j, i: (i, j)),
        out_shape=jax.ShapeDtypeStruct((m, n), F32),
        scratch_shapes=[pltpu.VMEM((ka + kb, tn), BF16)],
        compiler_params=_params(2),
        name="out_proj",
    )(ya, yb, w_out, w_out, x2d)


def _ln_kernel(z_ref, g_ref, b_ref, *o_refs):
    z = z_ref[...]
    mu = jnp.mean(z, axis=1, keepdims=True)
    zc = z - mu
    var = jnp.mean(zc * zc, axis=1, keepdims=True)
    y = zc * lax.rsqrt(var + LN_EPS) * g_ref[...] + b_ref[...]
    for o_ref in o_refs:
        o_ref[...] = y.astype(o_ref.dtype)


def _layer_norm(z, g, b, out_dtypes):
    m, d = z.shape
    tm = _pick(m, (512, 256, 128))
    row = pl.BlockSpec((tm, d), lambda i: (i, 0))
    vec = pl.BlockSpec((1, d), lambda i: (0, 0))
    return pl.pallas_call(
        _ln_kernel,
        grid=(m // tm,),
        in_specs=[row, vec, vec],
        out_specs=[row for _ in out_dtypes],
        out_shape=[jax.ShapeDtypeStruct((m, d), dt) for dt in out_dtypes],
        compiler_params=_params(1),
        name="layer_norm",
    )(z, g.reshape(1, d), b.reshape(1, d))


def _causal_conv(u, history, cw_ref, cb_ref):
    ext = jnp.concatenate([history, u], axis=0)
    out = None
    for j in range(CONV_WIDTH):
        back = CONV_WIDTH - 1 - j
        uj = u if back == 0 else pltpu.roll(ext, back, axis=0)[SUBLANES:]
        term = cw_ref[j:j + 1, :] * uj
        out = term if out is None else out + term
    return out + cb_ref[...]


def _ffn_up_kernel(x_ref, wv_ref, wg_ref, cwv_ref, cwg_ref, cbv_ref, cbg_ref, o_ref):
    wv = wv_ref[...].astype(BF16)
    wg = wg_ref[...].astype(BF16)
    n_chunks = x_ref.shape[0] // FFN_ROW_CHUNK

    def matmuls(c):
        x = x_ref[c * FFN_ROW_CHUNK:(c + 1) * FFN_ROW_CHUNK, :]
        return (jnp.dot(x, wv, preferred_element_type=F32), jnp.dot(x, wg, preferred_element_type=F32))

    pending = matmuls(0)
    history = (jnp.zeros((SUBLANES, wv.shape[1]), F32),) * 2
    for c in range(n_chunks):
        uv, ug = pending
        if c + 1 < n_chunks:
            pending = matmuls(c + 1)
        val = _causal_conv(uv, history[0], cwv_ref, cbv_ref)
        gate = _causal_conv(ug, history[1], cwg_ref, cbg_ref)
        o_ref[c * FFN_ROW_CHUNK:(c + 1) * FFN_ROW_CHUNK, :] = (
            val * (gate * (1.0 / (1.0 + jnp.exp(-gate))))).astype(BF16)
        history = (uv[-SUBLANES:], ug[-SUBLANES:])


def _ffn_up(x1_bf, w_up, conv_w, conv_b, batch, seq):
    m, d = x1_bf.shape
    dff = w_up.shape[1] // 2
    tn = _pick(dff, (256, 128))
    nj = dff // tn
    cb = conv_b.reshape(1, 2 * dff)
    return pl.pallas_call(
        _ffn_up_kernel,
        grid=(batch, nj),
        in_specs=[_resident((seq, d), lambda b, j: (b, 0)),
                  pl.BlockSpec((d, tn), lambda b, j: (0, j)),
                  pl.BlockSpec((d, tn), lambda b, j: (0, nj + j)),
                  pl.BlockSpec((CONV_WIDTH, tn), lambda b, j: (0, j)),
                  pl.BlockSpec((CONV_WIDTH, tn), lambda b, j: (0, nj + j)),
                  pl.BlockSpec((1, tn), lambda b, j: (0, j)),
                  pl.BlockSpec((1, tn), lambda b, j: (0, nj + j))],
        out_specs=pl.BlockSpec((seq, tn), lambda b, j: (b, j)),
        out_shape=jax.ShapeDtypeStruct((m, dff), BF16),
        compiler_params=_params(2),
        name="ffn_up",
    )(x1_bf, w_up, w_up, conv_w, conv_w, cb, cb)


def _ffn_down_kernel(h_ref, w_ref, x_ref, o_ref, wbf_ref):
    @pl.when(pl.program_id(1) == 0)
    def _():
        wbf_ref[...] = w_ref[...].astype(BF16)

    o_ref[...] = ALPHA * x_ref[...] + jnp.dot(h_ref[...], wbf_ref[...], preferred_element_type=F32)


def _ffn_down(h, w_down, x1):
    m, k = h.shape
    n = w_down.shape[1]
    tm = _pick(m, (512, 256))
    tn = _pick(n, (256, 128))
    return pl.pallas_call(
        _ffn_down_kernel,
        grid=(n // tn, m // tm),
        in_specs=[pl.BlockSpec((tm, k), lambda j, i: (i, 0)),
                  pl.BlockSpec((k, tn), lambda j, i: (0, j)),
                  pl.BlockSpec((tm, tn), lambda j, i: (i, j))],
        out_specs=pl.BlockSpec((tm, tn), lambda j, i: (i, j)),
        out_shape=jax.ShapeDtypeStruct((m, n), F32),
        scratch_shapes=[pltpu.VMEM((k, tn), BF16)],
        compiler_params=_params(2),
        name="ffn_down",
    )(h, w_down, x1)


def kernel(x, w_in, rel_table, g_moba, g_dsa, w_out, ln1_g, ln1_b, w_up, conv_w, conv_b, w_down,
           ln2_g, ln2_b):
    batch, seq, d = x.shape
    m = batch * seq
    nh = d // HEAD_DIM // 2
    wg = nh * HEAD_DIM
    n_main = 6 * wg
    assert w_in.shape == (d, n_main + N_IDX) and seq % TQ == 0 and seq // 4 >= DSA_TOPK

    x2d = x.reshape(m, d)
    x_bf = x2d.astype(BF16)
    w_t = w_in.T

    qk_heads = _proj_qk(x_bf, w_t, wg)
    vt_tiles = _proj_vt(x_bf, w_t, wg)
    qi, ki, wi = _proj_idx(x_bf, w_t, n_main)
    bias = _bias_tiles(rel_table)

    ya = _moba(rel_table, qk_heads, vt_tiles, bias, g_moba.reshape(1, wg), batch, seq, nh,
               0, nh, 0, 0)
    yb = _dsa(rel_table, qk_heads, vt_tiles, bias, qi, ki, wi, g_dsa.reshape(1, wg), batch, seq, nh,
              2 * nh, 3 * nh, nh, nh)

    z1 = _out_proj(ya, yb, w_out, x2d)
    x1, x1_bf = _layer_norm(z1, ln1_g, ln1_b, (F32, BF16))

    hmid = _ffn_up(x1_bf, w_up, conv_w, conv_b, batch, seq)
    z2 = _ffn_down(hmid, w_down, x1)
    (out,) = _layer_norm(z2, ln2_g, ln2_b, (F32,))
    return out.reshape(batch, seq, d)
```
